```python
import jax, jax.numpy as jnp
from jax import lax
import numpy as np

D_MODEL = 1024
BATCH = 16
SEQ = 2048
DEPTH = 1
DEC_BATCH = 32
DEC_SEQ = 1
PAST_LEN = 16384
PAGE_SIZE = 128

N_MEM = 256
ROPE_THETA = 10000.0
EPS = 1e-6
GLA_HEADS = 4
GLA_DK = D_MODEL // 16
GLA_DV = D_MODEL // 8
GLA_QK = GLA_HEADS * GLA_DK
GLA_WIDTH = GLA_HEADS * GLA_DV
GLA_GATE_RANK = 16
GLA_GATE_NORM = 16.0
GLA_CHUNK = 64
DSA_HEADS = 4
DSA_HEAD_DIM = D_MODEL // 16
DSA_WIDTH = DSA_HEADS * DSA_HEAD_DIM
IDX_HEADS = 4
IDX_DIM = 64
DSA_TOPK_MAX = 256
Q_BLOCK = 128
MEM_HEADS = 4
MEM_HEAD_DIM = D_MODEL // 16
MEM_WIDTH = MEM_HEADS * MEM_HEAD_DIM
MIX_WIDTH = GLA_WIDTH + DSA_WIDTH + MEM_WIDTH
COL_SIZES = (GLA_QK, GLA_QK, GLA_WIDTH, GLA_GATE_RANK, GLA_WIDTH,
             DSA_WIDTH, DSA_WIDTH, DSA_WIDTH, IDX_HEADS * IDX_DIM, IDX_DIM, IDX_HEADS, DSA_WIDTH,
             MEM_WIDTH, MEM_WIDTH)
D_IN = sum(COL_SIZES)

kernel_name = "hymba_gla_dsa_memory_step"


def rmsnorm(x, g):
    xf = x.astype(jnp.float32)
    y = xf * lax.rsqrt(jnp.mean(xf * xf, axis=-1, keepdims=True) + EPS)
    return (y * g.astype(jnp.float32)).astype(x.dtype)


def rope(x, pos):
    d = x.shape[-1]
    inv = ROPE_THETA ** (-jnp.arange(0, d, 2, dtype=jnp.float32) / d)
    ang = pos.astype(jnp.float32)[:, None] * inv[None, :]
    ang = ang.reshape((ang.shape[0],) + (1,) * (x.ndim - 3) + (d // 2,))
    cos, sin = jnp.cos(ang), jnp.sin(ang)
    xf = x.astype(jnp.float32)
    x1, x2 = xf[..., : d // 2], xf[..., d // 2:]
    return jnp.concatenate([x1 * cos - x2 * sin, x2 * cos + x1 * sin], axis=-1).astype(x.dtype)


def split_columns(z):
    offs, acc = [], 0
    for s in COL_SIZES[:-1]:
        acc += s
        offs.append(acc)
    return jnp.split(z, offs, axis=-1)


def project(h, g_in, w_in_l, w_g2, b_g, pos):
    B, T, _ = h.shape
    z = rmsnorm(h, g_in) @ w_in_l
    (gq, gk, gv, glr, ggate, dq, dk, dv, iq, ik, iw, dgate, mq, mgate) = split_columns(z)
    gla_q = gq.reshape(B, T, GLA_HEADS, GLA_DK) * (GLA_DK ** -0.5)
    gla_k = gk.reshape(B, T, GLA_HEADS, GLA_DK)
    gla_v = gv.reshape(B, T, GLA_HEADS, GLA_DV)
    gla_g = (jax.nn.log_sigmoid((glr @ w_g2 + b_g).astype(jnp.float32)) / GLA_GATE_NORM).reshape(B, T, GLA_HEADS, GLA_DK)
    q = rope(dq.reshape(B, T, DSA_HEADS, DSA_HEAD_DIM), pos)
    k = rope(dk.reshape(B, T, DSA_HEADS, DSA_HEAD_DIM), pos)
    v = dv.reshape(B, T, DSA_HEADS, DSA_HEAD_DIM)
    iq = rope(iq.reshape(B, T, IDX_HEADS, IDX_DIM), pos)
    ik = rope(ik, pos)
    iw = iw * (IDX_HEADS ** -0.5)
    mq = mq.reshape(B, T, MEM_HEADS, MEM_HEAD_DIM)
    return gla_q, gla_k, gla_v, gla_g, ggate, q, k, v, iq, ik, iw, dgate, mq, mgate


def gla_scan(q, k, v, g, s0):
    B, T, H, _ = q.shape
    C = GLA_CHUNK if T % GLA_CHUNK == 0 else T
    N = T // C
    chunks = lambda a: a.astype(jnp.float32).reshape(B, N, C, H, a.shape[-1]).transpose(1, 0, 3, 2, 4)
    mask = jnp.tril(jnp.ones((C, C), dtype=bool))

    def step(S, inp):
        qc, kc, vc, gc = inp
        b = jnp.cumsum(gc, axis=2)
        o_inter = jnp.einsum('bhtd,bhdv->bhtv', qc * jnp.exp(b), S)
        diff = b[:, :, :, None, :] - b[:, :, None, :, :]
        decay = jnp.exp(jnp.where(mask[None, None, :, :, None], diff, -jnp.inf))
        A = jnp.einsum('bhtd,bhsd,bhtsd->bhts', qc, kc, decay)
        o = o_inter + jnp.einsum('bhts,bhsv->bhtv', A, vc)
        b_last = b[:, :, -1:, :]
        S_new = jnp.exp(b_last[:, :, 0, :])[..., None] * S + jnp.einsum('bhsd,bhsv->bhdv', kc * jnp.exp(b_last - b), vc)
        return S_new, o

    S_fin, o = lax.scan(step, s0.astype(jnp.float32), (chunks(q), chunks(k), chunks(v), chunks(g)))
    o = o.transpose(1, 0, 3, 2, 4).reshape(B, T, H, v.shape[-1])
    return o.astype(v.dtype), S_fin.astype(v.dtype)


def index_scores(iq, ik, iw):
    r = jax.nn.relu(jnp.einsum('bqhd,bsd->bqhs', iq, ik).astype(jnp.float32))
    return jnp.einsum('bqh,bqhs->bqs', iw.astype(jnp.float32), r)


def sparse_attend(q, kg, vg, valid):
    s = jnp.einsum('bqhd,bqkhd->bqhk', q, kg).astype(jnp.float32) * (q.shape[-1] ** -0.5)
    p = jax.nn.softmax(jnp.where(valid[:, :, None, :], s, -jnp.inf), axis=-1)
    return jnp.einsum('bqhk,bqkhd->bqhd', p.astype(vg.dtype), vg)


def gather_rows(a, idx):
    return jax.vmap(lambda ab, ib: ab[ib])(a, idx)


def dsa_prompt(q, k, v, iq, ik, iw):
    B, T = q.shape[:2]
    topk = min(DSA_TOPK_MAX, T // 4)
    nb = T // Q_BLOCK
    kpos = jnp.arange(T)
    blocks = lambda a: a.reshape((B, nb, Q_BLOCK) + a.shape[2:]).swapaxes(0, 1)

    def one(args):
        qb, iqb, iwb, qpos = args
        sc = index_scores(iqb, ik, iwb)
        sc = jnp.where(kpos[None, None, :] <= qpos[None, :, None], sc, -jnp.inf)
        _, idx = lax.top_k(sc, topk)
        valid = idx <= qpos[None, :, None]
        return sparse_attend(qb, gather_rows(k, idx), gather_rows(v, idx), valid)

    out = lax.map(one, (blocks(q), blocks(iq), blocks(iw), kpos.reshape(nb, Q_BLOCK)))
    return out.swapaxes(0, 1).reshape(B, T, DSA_HEADS, DSA_HEAD_DIM)


def dsa_sample(q, k, v, iq, ik, iw, ck, cv, cik, page_table):
    Bd, Tn = q.shape[:2]
    past = page_table.shape[1] * PAGE_SIZE
    L = past + Tn
    topk = min(DSA_TOPK_MAX, L // 4)
    past_ik = cik[page_table].reshape(Bd, past, IDX_DIM)
    ik_all = jnp.concatenate([past_ik, ik], axis=1)
    qpos = past + jnp.arange(Tn)
    kpos = jnp.arange(L)
    sc = index_scores(iq, ik_all, iw)
    sc = jnp.where(kpos[None, None, :] <= qpos[None, :, None], sc, -jnp.inf)
    _, idx = lax.top_k(sc, topk)
    valid = idx <= qpos[None, :, None]
    in_past = (idx < past)[..., None, None]
    pidx = jnp.minimum(idx, past - 1)
    phys = jax.vmap(lambda pt, i: pt[i // PAGE_SIZE])(page_table, pidx) * PAGE_SIZE + pidx % PAGE_SIZE
    nidx = jnp.clip(idx - past, 0, Tn - 1)
    kg = jnp.where(in_past, ck.reshape(-1, DSA_HEADS, DSA_HEAD_DIM)[phys], gather_rows(k, nidx))
    vg = jnp.where(in_past, cv.reshape(-1, DSA_HEADS, DSA_HEAD_DIM)[phys], gather_rows(v, nidx))
    return sparse_attend(q, kg, vg, valid)


def mem_kv(mem, g, w):
    B = mem.shape[0]
    mk, mv = jnp.split(rmsnorm(mem, g) @ w, 2, axis=-1)
    shp = (B, mem.shape[1], MEM_HEADS, MEM_HEAD_DIM)
    return mk.reshape(shp), mv.reshape(shp)


def mem_attend(q, mk, mv):
    s = jnp.einsum('bthd,bmhd->bhtm', q, mk).astype(jnp.float32) * (MEM_HEAD_DIM ** -0.5)
    p = jax.nn.softmax(s, axis=-1)
    return jnp.einsum('bhtm,bmhd->bthd', p.astype(mv.dtype), mv)


def combine(gla_o, dsa_o, mem_o, ggate, dgate, mgate, onorm, w_out_l):
    B, T = gla_o.shape[:2]
    a = rmsnorm(gla_o, onorm).reshape(B, T, GLA_WIDTH) * jax.nn.silu(ggate)
    b = dsa_o.reshape(B, T, DSA_WIDTH) * jax.nn.silu(dgate)
    m = mem_o.reshape(B, T, MEM_WIDTH) * jax.nn.silu(mgate)
    return jnp.concatenate([a, b, m], axis=-1) @ w_out_l


def setup_inputs(seed: int = 0) -> dict:
    key = jax.random.key(seed)
    ks = jax.random.split(key, 20)
    n_pages = PAST_LEN // PAGE_SIZE
    n_used = DEC_BATCH * n_pages
    n_pool = (5 * n_used + 3) // 4
    nrm = lambda k, shape, scale=1.0: jax.random.normal(k, shape, jnp.float32) * scale
    return {
        "x_prompt": nrm(ks[0], (BATCH, SEQ, D_MODEL)),
        "x_sample": nrm(ks[1], (DEC_BATCH, DEC_SEQ, D_MODEL)),
        "mem_prompt": nrm(ks[2], (BATCH, N_MEM, D_MODEL)),
        "state_gla": nrm(ks[3], (DEPTH, DEC_BATCH, GLA_HEADS, GLA_DK, GLA_DV), 0.5),
        "cache_k": nrm(ks[4], (n_pool, DEPTH, PAGE_SIZE, DSA_HEADS, DSA_HEAD_DIM)),
        "cache_v": nrm(ks[5], (n_pool, DEPTH, PAGE_SIZE, DSA_HEADS, DSA_HEAD_DIM)),
        "cache_ik": nrm(ks[6], (n_pool, DEPTH, PAGE_SIZE, IDX_DIM)),
        "cache_mem_k": nrm(ks[7], (DEPTH, DEC_BATCH, N_MEM, MEM_HEADS, MEM_HEAD_DIM)),
        "cache_mem_v": nrm(ks[8], (DEPTH, DEC_BATCH, N_MEM, MEM_HEADS, MEM_HEAD_DIM)),
        "page_table": jax.random.permutation(ks[9], n_pool)[:n_used].reshape(DEC_BATCH, n_pages).astype(jnp.int32),
        "norm_in": 1.0 + nrm(ks[10], (DEPTH, D_MODEL), 0.01),
        "w_in": nrm(ks[11], (DEPTH, D_MODEL, D_IN), D_MODEL ** -0.5),
        "w_gla_g2": nrm(ks[12], (DEPTH, GLA_GATE_RANK, GLA_QK), GLA_GATE_RANK ** -0.5),
        "b_gla_g": nrm(ks[13], (DEPTH, GLA_QK), 0.01),
        "gla_onorm": 1.0 + nrm(ks[14], (DEPTH, GLA_DV), 0.01),
        "mem_norm": 1.0 + nrm(ks[15], (DEPTH, D_MODEL), 0.01),
        "w_mem_kv": nrm(ks[16], (DEPTH, D_MODEL, 2 * MEM_WIDTH), D_MODEL ** -0.5),
        "w_out": nrm(ks[17], (DEPTH, MIX_WIDTH, D_MODEL), MIX_WIDTH ** -0.5),
        "norm_final": 1.0 + nrm(ks[18], (D_MODEL,), 0.01),
    }


def reference(x_prompt, x_sample, mem_prompt, state_gla, cache_k, cache_v, cache_ik, cache_mem_k, cache_mem_v,
              page_table, norm_in, w_in, w_gla_g2, b_gla_g, gla_onorm, mem_norm, w_mem_kv, w_out, norm_final):
    Bp, Tp, _ = x_prompt.shape
    Bs, Ts, _ = x_sample.shape
    past = page_table.shape[1] * PAGE_SIZE
    pos_p = jnp.arange(Tp)
    pos_s = past + jnp.arange(Ts)
    hp, hs = x_prompt, x_sample
    gla_p, k_p, v_p, ik_p, mk_p, mv_p = [], [], [], [], [], []
    gla_s, k_s, v_s, ik_s = [], [], [], []
    for l in range(DEPTH):
        (gq, gk, gv, gg, ggate, q, k, v, iq, ik, iw, dgate, mq, mgate) = project(
            hp, norm_in[l], w_in[l], w_gla_g2[l], b_gla_g[l], pos_p)
        s0 = jnp.zeros((Bp, GLA_HEADS, GLA_DK, GLA_DV), jnp.float32)
        gla_o, S_p = gla_scan(gq, gk, gv, gg, s0)
        dsa_o = dsa_prompt(q, k, v, iq, ik, iw)
        mk, mv = mem_kv(mem_prompt, mem_norm[l], w_mem_kv[l])
        mem_o = mem_attend(mq, mk, mv)
        hp = hp + combine(gla_o, dsa_o, mem_o, ggate, dgate, mgate, gla_onorm[l], w_out[l])
        gla_p.append(S_p); k_p.append(k); v_p.append(v); ik_p.append(ik); mk_p.append(mk); mv_p.append(mv)
        (gq, gk, gv, gg, ggate, q, k, v, iq, ik, iw, dgate, mq, mgate) = project(
            hs, norm_in[l], w_in[l], w_gla_g2[l], b_gla_g[l], pos_s)
        gla_o, S_s = gla_scan(gq, gk, gv, gg, state_gla[l])
        dsa_o = dsa_sample(q, k, v, iq, ik, iw, cache_k[:, l], cache_v[:, l], cache_ik[:, l], page_table)
        mem_o = mem_attend(mq, cache_mem_k[l], cache_mem_v[l])
        hs = hs + combine(gla_o, dsa_o, mem_o, ggate, dgate, mgate, gla_onorm[l], w_out[l])
        gla_s.append(S_s); k_s.append(k); v_s.append(v); ik_s.append(ik)
    y_prompt = rmsnorm(hp, norm_final)
    y_sample = rmsnorm(hs, norm_final)
    return (y_prompt, y_sample,
            jnp.stack(gla_p, axis=0), jnp.stack(k_p, axis=1), jnp.stack(v_p, axis=1), jnp.stack(ik_p, axis=1),
            jnp.stack(mk_p, axis=0), jnp.stack(mv_p, axis=0),
            jnp.stack(gla_s, axis=0), jnp.stack(k_s, axis=1), jnp.stack(v_s, axis=1), jnp.stack(ik_s, axis=1))
```

```python
import functools

import jax
import jax.numpy as jnp
import numpy as np
from jax import lax
from jax.experimental import pallas as pl
from jax.experimental.pallas import tpu as pltpu

F32 = jnp.float32
BF16 = jnp.bfloat16
I32 = jnp.int32

EPS = 1e-6
ROPE_THETA = 10000.0
HEADS = 4
HD = 64
GLA_DV = 128
QKW = HEADS * HD
GVW = HEADS * GLA_DV
GATE_RANK = 16
GATE_NORM = 16.0
GLA_CHUNK = 64
GLA_SUB = 16
TOPK_MAX = 256
PAGE = 128
NEG = -1e30
INT_MIN = -(2 ** 31)
INT_MAX = 2 ** 31 - 1
VMEM_LIMIT = 56 * 1024 * 1024


def _nn(a, b):
    return jnp.dot(a, b, preferred_element_type=F32)


def _nt(a, b):
    return lax.dot_general(a, b, (((1,), (1,)), ((), ())), preferred_element_type=F32)


def _split3(x):
    hi = x.astype(BF16)
    r1 = x - hi.astype(F32)
    mid = r1.astype(BF16)
    lo = (r1 - mid.astype(F32)).astype(BF16)
    return hi, mid, lo


def _head_masks(width, rows=1):
    lane = lax.broadcasted_iota(I32, (rows, HEADS * width), 1)
    return [jnp.where((lane >= h * width) & (lane < (h + 1) * width), 1.0, 0.0).astype(F32)
            for h in range(HEADS)]


def _sortable(x):
    u = lax.bitcast_convert_type(x, I32)
    return jnp.where(u >= 0, u, u ^ INT_MAX)


def _rms(x, g):
    return x * lax.rsqrt(jnp.mean(x * x, axis=-1, keepdims=True) + EPS) * g


def _silu(x):
    return x * (1.0 / (1.0 + jnp.exp(-x)))


def _params(sem, vmem=VMEM_LIMIT):
    return pltpu.CompilerParams(dimension_semantics=sem, vmem_limit_bytes=vmem)


_SEGS = (("gq", QKW), ("gk", QKW), ("gv", GVW), ("ggate", GVW), ("q", QKW), ("k", QKW), ("v", QKW),
         ("iq", QKW), ("ik4", QKW), ("dgate", QKW), ("mq", QKW), ("mgate", QKW), ("tail", 128))
_SEG_OFF = {}
_o = 0
for _n, _w in _SEGS:
    _SEG_OFF[_n] = (_o, _w)
    _o += _w
W_CAT = _o
TAIL_IW = GATE_RANK


def _proj_body(x_ref, gin_ref, w_ref, wvt_ref, w2_ref, bg_ref, cos_ref, sa_ref, sb_ref,
               gq_o, gk_o, gv_o, gvt_o, gg_o, ggate_o, q_o, k_o, v_o, iq_o, ik_o, ik4_o, tail_o,
               dgate_o, mq_o, mgate_o):
    xn = _rms(x_ref[...], gin_ref[...]).astype(BF16)

    def seg(name):
        off, width = _SEG_OFF[name]
        return _nn(xn, w_ref[:, off:off + width])

    cos, sa, sb = cos_ref[...], sa_ref[...], sb_ref[...]

    def rope(z):
        return z * cos + pltpu.roll(z, QKW - HD // 2, 1) * sa + pltpu.roll(z, HD // 2, 1) * sb

    gq_o[...] = seg("gq") * (HD ** -0.5)
    gk_o[...] = seg("gk")
    gv_o[...] = seg("gv")
    gvt_o[...] = _nt(wvt_ref[...], xn)
    ggate_o[...] = seg("ggate")
    q_o[...] = rope(seg("q")) * (HD ** -0.5)
    k_o[...] = rope(seg("k"))
    v_o[...] = seg("v")
    iq_o[...] = rope(seg("iq"))
    ik4 = rope(seg("ik4"))
    ik4_o[...] = ik4
    ik_o[...] = ik4[:, :HD]
    dgate_o[...] = seg("dgate")
    mq_o[...] = seg("mq") * (HD ** -0.5)
    mgate_o[...] = seg("mgate")
    tail = seg("tail")
    tail_o[...] = tail
    pre = _nn(tail.astype(BF16), w2_ref[...]) + bg_ref[...]
    gg_o[...] = -(jnp.maximum(-pre, 0.0) + jnp.log(1.0 + jnp.exp(-jnp.abs(pre)))) * (1.0 / GATE_NORM)


def _project(x2d, g_in, w_cat, wvt, w2pad, bg, cos, sa, sb, tm):
    n, d = x2d.shape
    tab_blocks = cos.shape[0] // tm
    row = lambda w: pl.BlockSpec((tm, w), lambda i: (i, 0))
    const = lambda a: pl.BlockSpec(a.shape, lambda i: (0,) * a.ndim)
    tab = pl.BlockSpec((tm, QKW), lambda i: (i % tab_blocks, 0))
    widths = dict(gq=QKW, gk=QKW, gv=GVW, gg=QKW, ggate=GVW, q=QKW, k=QKW, v=QKW, iq=QKW, ik=HD,
                  ik4=QKW, tail=128, dgate=QKW, mq=QKW, mgate=QKW)
    order = ("gq", "gk", "gv", "gvt", "gg", "ggate", "q", "k", "v", "iq", "ik", "ik4", "tail",
             "dgate", "mq", "mgate")
    out_shape, out_specs = [], []
    for name in order:
        if name == "gvt":
            out_shape.append(jax.ShapeDtypeStruct((GVW, n), F32))
            out_specs.append(pl.BlockSpec((GVW, tm), lambda i: (0, i)))
        else:
            out_shape.append(jax.ShapeDtypeStruct((n, widths[name]), F32))
            out_specs.append(row(widths[name]))
    outs = pl.pallas_call(
        _proj_body,
        grid=(n // tm,),
        in_specs=[row(d), const(g_in), const(w_cat), const(wvt), const(w2pad), const(bg), tab, tab, tab],
        out_specs=out_specs,
        out_shape=out_shape,
        compiler_params=_params(("parallel",)),
        name="proj",
    )(x2d, g_in, w_cat, wvt, w2pad, bg, cos, sa, sb)
    return dict(zip(order, outs))


def _gla_body(q_ref, k_ref, g_ref, v_ref, vt_ref, lblk_ref, e_ref, mst_ref, o_ref, st_ref,
              s_ref, kpad, bpad, vpad, *, tg):
    t = pl.program_id(1)
    c_sz, sub = GLA_CHUNK, GLA_SUB

    @pl.when(t == 0)
    def _():
        s_ref[...] = jnp.zeros_like(s_ref)
        kpad[0:sub, :] = jnp.zeros((sub, QKW), F32)
        bpad[0:sub, :] = jnp.zeros((sub, QKW), F32)
        vpad[0:sub, :] = jnp.zeros((sub, GVW), F32)

    lblk = lblk_ref[...]
    b_all = sum(_nn(lblk, term) for term in _split3(g_ref[...]))
    kpad[sub:sub + tg, :] = k_ref[...]
    bpad[sub:sub + tg, :] = b_all
    vpad[sub:sub + tg, :] = v_ref[...]

    hm = _head_masks(HD)
    row = lax.broadcasted_iota(I32, (c_sz, QKW), 0)
    rowmod = row & (sub - 1)
    lane_t = lax.broadcasted_iota(I32, (1, tg), 1)
    e_mat = e_ref[...]
    mst = mst_ref[...]

    for c in range(tg // c_sz):
        r0 = c * c_sz
        q = q_ref[r0:r0 + c_sz, :]
        k = k_ref[r0:r0 + c_sz, :]
        v = v_ref[r0:r0 + c_sz, :]
        b = b_all[r0:r0 + c_sz]
        blast = b[c_sz - 1:c_sz]
        v16 = v.astype(BF16)

        o = _nt((q * jnp.exp(b)).astype(BF16), s_ref[...].astype(BF16))

        blocks = [jnp.zeros((sub, GVW), F32)]
        for i in range(1, c_sz // sub):
            bref = b[i * sub - 1:i * sub]
            qi = q[i * sub:(i + 1) * sub] * jnp.exp(b[i * sub:(i + 1) * sub] - bref)
            kp = jnp.where(row < i * sub, k * jnp.exp(jnp.minimum(bref - b, 0.0)), 0.0)
            qs = jnp.concatenate([qi * hm[h] for h in range(HEADS)], axis=0)
            a = _nt(qs.astype(BF16), kp.astype(BF16))
            r = _nn(a.astype(BF16), v16)
            blocks.append(jnp.concatenate(
                [r[h * sub:(h + 1) * sub, h * GLA_DV:(h + 1) * GLA_DV] for h in range(HEADS)], axis=1))
        o = o + jnp.concatenate(blocks, axis=0)

        for d in range(sub):
            lo = sub + r0 - d
            kd = kpad[lo:lo + c_sz, :]
            bd = bpad[lo:lo + c_sz, :]
            vd = vpad[lo:lo + c_sz, :]
            m = rowmod >= d
            w = jnp.where(m, q * kd * jnp.exp(jnp.where(m, b - bd, 0.0)), 0.0)
            o = o + _nn(w.astype(BF16), e_mat) * vd
        o_ref[r0:r0 + c_sz, :] = o

        in_chunk = jnp.where((lane_t >= r0) & (lane_t < r0 + c_sz), 1.0, 0.0)
        vtm = (vt_ref[...] * in_chunk).astype(BF16)
        kpp = (k_ref[...] * jnp.exp(jnp.minimum(blast - b_all, 0.0))).astype(BF16)
        s_ref[...] = s_ref[...] * jnp.exp(blast) + mst * _nn(vtm, kpp)

    @pl.when(t == pl.num_programs(1) - 1)
    def _():
        st_ref[...] = s_ref[...]


def _gla_prompt(p, batch, seq, tg):
    n = batch * seq
    nt = seq // tg
    r = np.arange(tg)
    lblk = ((r[:, None] // GLA_CHUNK == r[None, :] // GLA_CHUNK) & (r[None, :] <= r[:, None]))
    lblk = jnp.asarray(lblk, BF16)
    e_mat = jnp.asarray(np.arange(QKW)[:, None] // HD == np.arange(GVW)[None, :] // GLA_DV, BF16)
    mst = jnp.asarray(np.arange(GVW)[:, None] // GLA_DV == np.arange(QKW)[None, :] // HD, F32)
    row = lambda w: pl.BlockSpec((tg, w), lambda b, t: (b * nt + t, 0))
    const = lambda a: pl.BlockSpec(a.shape, lambda b, t: (0,) * a.ndim)
    return pl.pallas_call(
        functools.partial(_gla_body, tg=tg),
        grid=(batch, nt),
        in_specs=[row(QKW), row(QKW), row(QKW), row(GVW),
                  pl.BlockSpec((GVW, tg), lambda b, t: (0, b * nt + t)),
                  const(lblk), const(e_mat), const(mst)],
        out_specs=[row(GVW), pl.BlockSpec((None, GVW, QKW), lambda b, t: (b, 0, 0))],
        out_shape=[jax.ShapeDtypeStruct((n, GVW), F32), jax.ShapeDtypeStruct((batch, GVW, QKW), F32)],
        scratch_shapes=[pltpu.VMEM((GVW, QKW), F32), pltpu.VMEM((GLA_SUB + tg, QKW), F32),
                        pltpu.VMEM((GLA_SUB + tg, QKW), F32), pltpu.VMEM((GLA_SUB + tg, GVW), F32)],
        compiler_params=_params(("parallel", "arbitrary")),
        name="gla_prompt",
    )(p["gq"], p["gk"], p["gg"], p["gv"], p["gvt"], lblk, e_mat, mst)


def _ind(cond):
    return jnp.where(cond, 1.0, 0.0)


def _kth_threshold(count, kf):
    c0 = count(lambda key, idx: _ind(key >= 0))
    thr = jnp.where(c0 >= kf, 0, INT_MIN).astype(I32)

    def bit(i, thr):
        cand = thr + jnp.left_shift(jnp.int32(1), 30 - i)
        c = count(lambda key, idx: _ind(key >= cand))
        return jnp.where(c >= kf, cand, thr)

    return lax.fori_loop(0, 31, bit, thr)


def _tie_cutoff(count, thr, need, idx_bits):
    cut = jnp.zeros_like(thr)
    for bit in range(idx_bits - 1, -1, -1):
        cand = cut + (1 << bit)
        c = count(lambda key, idx: jnp.where(key == thr, _ind(idx < cand), 0.0))
        cut = jnp.where(c < need, cand, cut)
    return cut


def _selected(key, idx, thr, cut):
    return jnp.where(key > thr, 1.0, jnp.where(key == thr, _ind(idx <= cut), 0.0))


def _dsa_body(q_ref, iq_ref, tail_ref, k_ref, v_ref, ik4_ref, o_ref,
              keys, thr_ref, cut_ref, m_s, l_s, acc_s, *, tq, topk, idx_bits):
    j = pl.program_id(1)
    nkb = j + 1
    kf = float(topk)
    hm = _head_masks(HD)
    qpos = j * tq + lax.broadcasted_iota(I32, (tq, tq), 0)
    lane = lax.broadcasted_iota(I32, (tq, tq), 1)

    iq = iq_ref[...]
    iqh = [(iq * hm[h]).astype(BF16) for h in range(HEADS)]
    iw = [tail_ref[:, TAIL_IW + h:TAIL_IW + h + 1] * (HEADS ** -0.5) for h in range(HEADS)]

    def score_block(kb, carry):
        off = pl.multiple_of(kb * tq, tq)
        ikb = ik4_ref[pl.ds(off, tq), :].astype(BF16)
        sc = jnp.zeros((tq, tq), F32)
        for h in range(HEADS):
            sc = sc + jnp.maximum(_nt(iqh[h], ikb), 0.0) * iw[h]
        sc = jnp.where(kb * tq + lane <= qpos, sc, -jnp.inf)
        keys[:, pl.ds(off, tq)] = _sortable(sc)
        return carry

    lax.fori_loop(0, nkb, score_block, 0)

    def count(pred):
        def body(kb, acc):
            off = pl.multiple_of(kb * tq, tq)
            m = pred(keys[:, pl.ds(off, tq)], kb * tq + lane)
            for c in range(tq // 128):
                acc = acc + m[:, c * 128:(c + 1) * 128]
            return acc
        acc = lax.fori_loop(0, nkb, body, jnp.zeros((tq, 128), F32))
        return jnp.sum(acc, axis=-1, keepdims=True)

    thr_ref[...] = jnp.full((tq, 1), INT_MIN, I32)
    cut_ref[...] = jnp.full((tq, 1), INT_MAX, I32)

    @pl.when(nkb * tq > topk)
    def _():
        thr = _kth_threshold(count, kf)
        thr_ref[...] = thr
        c_ge = count(lambda key, idx: _ind(key >= thr))

        @pl.when(jnp.max(c_ge) > kf)
        def _():
            need = kf - count(lambda key, idx: _ind(key > thr))
            cut_ref[...] = _tie_cutoff(count, thr, need, idx_bits)

    thr = thr_ref[...]
    cut = cut_ref[...]
    q = q_ref[...]
    qh = [(q * hm[h]).astype(BF16) for h in range(HEADS)]
    m_s[...] = jnp.full(m_s.shape, NEG, F32)
    l_s[...] = jnp.zeros(l_s.shape, F32)
    acc_s[...] = jnp.zeros(acc_s.shape, F32)

    def attend_block(kb, carry):
        off = pl.multiple_of(kb * tq, tq)
        kb16 = k_ref[pl.ds(off, tq), :].astype(BF16)
        vb = v_ref[pl.ds(off, tq), :]
        idx = kb * tq + lane
        sel = jnp.where(idx <= qpos, _selected(keys[:, pl.ds(off, tq)], idx, thr, cut), 0.0) > 0.5
        ps = []
        alpha = jnp.zeros((tq, QKW), F32)
        for h in range(HEADS):
            s = jnp.where(sel, _nt(qh[h], kb16), NEG)
            m_old = m_s[h]
            m_new = jnp.maximum(m_old, jnp.max(s, axis=-1, keepdims=True))
            a = jnp.exp(m_old - m_new)
            p = jnp.where(sel, jnp.exp(s - m_new), 0.0)
            l_s[h] = a * l_s[h] + jnp.sum(p, axis=-1, keepdims=True)
            m_s[h] = m_new
            alpha = alpha + a * hm[h]
            ps.append(p.astype(BF16))
        pcat = jnp.concatenate(ps, axis=1)
        vbd = jnp.concatenate([(vb * hm[h]).astype(BF16) for h in range(HEADS)], axis=0)
        acc_s[...] = acc_s[...] * alpha + _nn(pcat, vbd)
        return carry

    lax.fori_loop(0, nkb, attend_block, 0)
    inv = jnp.zeros((tq, QKW), F32)
    for h in range(HEADS):
        inv = inv + hm[h] * (1.0 / l_s[h])
    o_ref[...] = acc_s[...] * inv


def _dsa_prompt(p, batch, seq, tq):
    n = batch * seq
    nq = seq // tq
    topk = min(TOPK_MAX, seq // 4)
    row = lambda w: pl.BlockSpec((tq, w), lambda b, j: (b * nq + j, 0))
    full = lambda w: pl.BlockSpec((seq, w), lambda b, j: (b, 0))
    return pl.pallas_call(
        functools.partial(_dsa_body, tq=tq, topk=topk, idx_bits=max(1, (seq - 1).bit_length())),
        grid=(batch, nq),
        in_specs=[row(QKW), row(QKW), row(128), full(QKW), full(QKW), full(QKW)],
        out_specs=row(QKW),
        out_shape=jax.ShapeDtypeStruct((n, QKW), F32),
        scratch_shapes=[pltpu.VMEM((tq, seq), I32), pltpu.VMEM((tq, 1), I32), pltpu.VMEM((tq, 1), I32),
                        pltpu.VMEM((HEADS, tq, 1), F32), pltpu.VMEM((HEADS, tq, 1), F32),
                        pltpu.VMEM((tq, QKW), F32)],
        compiler_params=_params(("parallel", "arbitrary")),
        name="dsa_prompt",
    )(p["q"], p["iq"], p["tail"], p["k"], p["v"], p["ik4"])


def _memkv_body(x_ref, g_ref, w_ref, mk_o, mv_o):
    xn = _rms(x_ref[...], g_ref[...]).astype(BF16)
    mk_o[...] = _nn(xn, w_ref[:, :QKW])
    mv_o[...] = _nn(xn, w_ref[:, QKW:])


def _mem_kv(mem2d, g, w, tm):
    n, d = mem2d.shape
    return pl.pallas_call(
        _memkv_body,
        grid=(n // tm,),
        in_specs=[pl.BlockSpec((tm, d), lambda i: (i, 0)), pl.BlockSpec(g.shape, lambda i: (0, 0)),
                  pl.BlockSpec(w.shape, lambda i: (0, 0))],
        out_specs=[pl.BlockSpec((tm, QKW), lambda i: (i, 0))] * 2,
        out_shape=[jax.ShapeDtypeStruct((n, QKW), F32)] * 2,
        compiler_params=_params(("parallel",)),
        name="mem_kv",
    )(mem2d, g, w)


def _memattn_body(q_ref, mk_ref, mv_ref, o_ref):
    hm = _head_masks(HD)
    q = q_ref[...]
    mk = mk_ref[...].astype(BF16)
    mv = mv_ref[...]
    ps = []
    inv = jnp.zeros(q.shape, F32)
    for h in range(HEADS):
        s = _nt((q * hm[h]).astype(BF16), mk)
        p = jnp.exp(s - jnp.max(s, axis=-1, keepdims=True))
        inv = inv + hm[h] * (1.0 / jnp.sum(p, axis=-1, keepdims=True))
        ps.append(p.astype(BF16))
    vbd = jnp.concatenate([(mv * hm[h]).astype(BF16) for h in range(HEADS)], axis=0)
    o_ref[...] = _nn(jnp.concatenate(ps, axis=1), vbd) * inv


def _mem_attend_prompt(mq, mk, mv, batch, seq, n_mem, tm):
    nt = seq // tm
    return pl.pallas_call(
        _memattn_body,
        grid=(batch, nt),
        in_specs=[pl.BlockSpec((tm, QKW), lambda b, t: (b * nt + t, 0)),
                  pl.BlockSpec((n_mem, QKW), lambda b, t: (b, 0)),
                  pl.BlockSpec((n_mem, QKW), lambda b, t: (b, 0))],
        out_specs=pl.BlockSpec((tm, QKW), lambda b, t: (b * nt + t, 0)),
        out_shape=jax.ShapeDtypeStruct((batch * seq, QKW), F32),
        compiler_params=_params(("parallel", "parallel")),
        name="mem_attend",
    )(mq, mk, mv)


def _combine_body(gla_ref, ggate_ref, dsa_ref, dgate_ref, mem_ref, mgate_ref, h_ref, on_ref, wo_ref,
                  nf_ref, y_ref):
    gla = gla_ref[...]
    onorm = on_ref[...]
    a = jnp.concatenate([_rms(gla[:, h * GLA_DV:(h + 1) * GLA_DV], onorm) for h in range(HEADS)], axis=1)
    a = (a * _silu(ggate_ref[...])).astype(BF16)
    b = (dsa_ref[...] * _silu(dgate_ref[...])).astype(BF16)
    m = (mem_ref[...] * _silu(mgate_ref[...])).astype(BF16)
    out = (_nn(a, wo_ref[0:GVW, :]) + _nn(b, wo_ref[GVW:GVW + QKW, :])
           + _nn(m, wo_ref[GVW + QKW:GVW + 2 * QKW, :]))
    y_ref[...] = _rms(h_ref[...] + out, nf_ref[...])


def _combine(gla_o, ggate, dsa_o, dgate, mem_o, mgate, h2d, onorm, wo, nf, tm):
    n, d = h2d.shape
    row = lambda w: pl.BlockSpec((tm, w), lambda i: (i, 0))
    const = lambda a: pl.BlockSpec(a.shape, lambda i: (0,) * a.ndim)
    return pl.pallas_call(
        _combine_body,
        grid=(n // tm,),
        in_specs=[row(GVW), row(GVW), row(QKW), row(QKW), row(QKW), row(QKW), row(d),
                  const(onorm), const(wo), const(nf)],
        out_specs=row(d),
        out_shape=jax.ShapeDtypeStruct((n, d), F32),
        compiler_params=_params(("parallel",)),
        name="combine",
    )(gla_o, ggate, dsa_o, dgate, mem_o, mgate, h2d, onorm, wo, nf)


def _decode_misc_body(s_ref, qc_ref, kc_ref, gc_ref, v_ref, mq_ref, mk_ref, mv_ref,
                      o_ref, sn_ref, mo_ref):
    s = s_ref[...]
    qc, kc, eg = qc_ref[...], kc_ref[...], jnp.exp(gc_ref[...])
    outs = []
    for h in range(HEADS):
        rows = slice(h * HD, (h + 1) * HD)
        vrow = v_ref[:, h * GLA_DV:(h + 1) * GLA_DV]
        sh = s[rows]
        sn_ref[rows, :] = eg[rows] * sh + kc[rows] * vrow
        o_inter = jnp.sum((qc[rows] * eg[rows]) * sh, axis=0, keepdims=True)
        a = jnp.sum(qc[rows] * kc[rows], axis=0, keepdims=True)
        outs.append(o_inter + a * vrow)
    o_ref[...] = jnp.concatenate(outs, axis=1)

    hm16 = _stacked_head_mask()
    qs = (mq_ref[...] * hm16).astype(BF16)
    sc = _nt(qs, mk_ref[...].astype(BF16))
    p = jnp.exp(sc - jnp.max(sc, axis=-1, keepdims=True))
    r = _nn(p.astype(BF16), mv_ref[...].astype(BF16)) / jnp.sum(p, axis=-1, keepdims=True)
    mo_ref[...] = jnp.sum(r * hm16, axis=0, keepdims=True)


def _stacked_head_mask(rows=16):
    r = lax.broadcasted_iota(I32, (rows, QKW), 0)
    lane = lax.broadcasted_iota(I32, (rows, QKW), 1)
    return jnp.where((lane >= r * HD) & (lane < (r + 1) * HD), 1.0, 0.0).astype(F32)


def _decode_misc(state, ps, mem_k, mem_v):
    bd = state.shape[0]
    n_mem = mem_k.shape[1]
    col = lambda a: a.reshape(bd, QKW, 1)
    cspec = pl.BlockSpec((None, QKW, 1), lambda b: (b, 0, 0))
    rspec = lambda w: pl.BlockSpec((None, 1, w), lambda b: (b, 0, 0))
    sspec = pl.BlockSpec((None, QKW, GLA_DV), lambda b: (b, 0, 0))
    mspec = pl.BlockSpec((None, n_mem, QKW), lambda b: (b, 0, 0))
    return pl.pallas_call(
        _decode_misc_body,
        grid=(bd,),
        in_specs=[sspec, cspec, cspec, cspec, rspec(GVW), rspec(QKW), mspec, mspec],
        out_specs=[rspec(GVW), sspec, rspec(QKW)],
        out_shape=[jax.ShapeDtypeStruct((bd, 1, GVW), F32), jax.ShapeDtypeStruct((bd, QKW, GLA_DV), F32),
                   jax.ShapeDtypeStruct((bd, 1, QKW), F32)],
        compiler_params=_params(("parallel",)),
        name="decode_gla_mem",
    )(state, col(ps["gq"]), col(ps["gk"]), col(ps["gg"]), ps["gv"].reshape(bd, 1, GVW),
      ps["mq"].reshape(bd, 1, QKW), mem_k, mem_v)


def _decode_select_body(pt_ref, iq_ref, iw_ref, ikn_ref, *rest, pg, past, topk, idx_bits):
    pages = rest[:pg]
    sel_ref, sc = rest[pg], rest[pg + 1]
    s = pl.program_id(1)
    qs = iq_ref[...]
    qs16 = qs.astype(BF16)
    iw = iw_ref[...]
    for i in range(pg):
        r = jnp.maximum(_nt(qs16, pages[i][...].astype(BF16)), 0.0)
        off = pl.multiple_of((s * pg + i) * PAGE, PAGE)
        sc[:, pl.ds(off, PAGE)] = jnp.sum(r * iw, axis=0, keepdims=True)

    @pl.when(s == pl.num_programs(1) - 1)
    def _():
        rn = jnp.maximum(jnp.sum(qs * ikn_ref[...], axis=-1, keepdims=True), 0.0)
        new = jnp.sum(rn * iw, axis=0, keepdims=True)
        lane = lax.broadcasted_iota(I32, (1, PAGE), 1)
        sc[:, past:past + PAGE] = jnp.where(lane == 0, new, -jnp.inf)
        key = _sortable(sc[...])
        idx = lax.broadcasted_iota(I32, key.shape, 1)
        kf = float(topk)

        def count(pred):
            return jnp.sum(pred(key, idx), axis=-1, keepdims=True)

        thr = _kth_threshold(count, kf)
        need = kf - count(lambda kk, ii: _ind(kk > thr))
        cut = _tie_cutoff(count, thr, need, idx_bits)
        sel_ref[...] = _selected(key, idx, thr, cut)


def _decode_select(ps, cache_ik, page_table, pg):
    bd, n_pages = page_table.shape
    past = n_pages * PAGE
    total = past + PAGE
    topk = min(TOPK_MAX, (past + 1) // 4)
    pad = lambda a: jnp.pad(a, ((0, 0), (0, 16 - HEADS), (0, 0)))
    iq16 = pad(ps["iq"].reshape(bd, HEADS, HD))
    iw16 = pad((ps["tail"][:, TAIL_IW:TAIL_IW + HEADS] * (HEADS ** -0.5)).reshape(bd, HEADS, 1))
    page_spec = lambda i: pl.BlockSpec((None, None, PAGE, HD),
                                       lambda b, s, pt: (pt[b, s * pg + i], 0, 0, 0))
    grid_spec = pltpu.PrefetchScalarGridSpec(
        num_scalar_prefetch=1,
        grid=(bd, n_pages // pg),
        in_specs=[pl.BlockSpec((None, 16, HD), lambda b, s, pt: (b, 0, 0)),
                  pl.BlockSpec((None, 16, 1), lambda b, s, pt: (b, 0, 0)),
                  pl.BlockSpec((None, 1, HD), lambda b, s, pt: (b, 0, 0))]
                 + [page_spec(i) for i in range(pg)],
        out_specs=pl.BlockSpec((None, 1, total), lambda b, s, pt: (b, 0, 0)),
        scratch_shapes=[pltpu.VMEM((1, total), F32)],
    )
    return pl.pallas_call(
        functools.partial(_decode_select_body, pg=pg, past=past, topk=topk,
                          idx_bits=max(1, (total - 1).bit_length())),
        grid_spec=grid_spec,
        out_shape=jax.ShapeDtypeStruct((bd, 1, total), F32),
        compiler_params=_params(("parallel", "arbitrary")),
        name="decode_select",
    )(page_table, iq16, iw16, ps["ik"].reshape(bd, 1, HD), *([cache_ik] * pg))


def _decode_attend_body(pt_ref, q_ref, kn_ref, vn_ref, sel_ref, seln_ref, *rest, pg):
    kpages, vpages = rest[:pg], rest[pg:2 * pg]
    o_ref, m_s, l_s, acc_s = rest[2 * pg:]
    s = pl.program_id(1)
    hm16 = _stacked_head_mask()
    qf = q_ref[...] * hm16
    qs = qf.astype(BF16)

    @pl.when(s == 0)
    def _():
        m_s[...] = jnp.full(m_s.shape, NEG, F32)
        l_s[...] = jnp.zeros(l_s.shape, F32)
        acc_s[...] = jnp.zeros(acc_s.shape, F32)

    def update(sco, sel, pv):
        sco = jnp.where(sel, sco, NEG)
        m_old = m_s[...]
        m_new = jnp.maximum(m_old, jnp.max(sco, axis=-1, keepdims=True))
        a = jnp.exp(m_old - m_new)
        p = jnp.where(sel, jnp.exp(sco - m_new), 0.0)
        l_s[...] = a * l_s[...] + jnp.sum(p, axis=-1, keepdims=True)
        m_s[...] = m_new
        acc_s[...] = acc_s[...] * a + pv(p)

    for i in range(pg):
        sel = sel_ref[:, i * PAGE:(i + 1) * PAGE] > 0.5
        vp = vpages[i][...].astype(BF16)
        update(_nt(qs, kpages[i][...].astype(BF16)), sel, lambda p: _nn(p.astype(BF16), vp))

    @pl.when(s == pl.num_programs(1) - 1)
    def _():
        sco = jnp.sum(qf * kn_ref[...], axis=-1, keepdims=True)
        update(sco, seln_ref[:, 0:1] > 0.5, lambda p: p * vn_ref[...])
        o_ref[...] = jnp.sum(acc_s[...] * hm16 / l_s[...], axis=0, keepdims=True)


def _decode_attend(ps, sel, cache_k, cache_v, page_table, pg):
    bd, n_pages = page_table.shape
    past = n_pages * PAGE
    row = lambda w: pl.BlockSpec((None, 1, w), lambda b, s, pt: (b, 0, 0))
    page_spec = lambda i: pl.BlockSpec((None, None, PAGE, QKW),
                                       lambda b, s, pt: (pt[b, s * pg + i], 0, 0, 0))
    grid_spec = pltpu.PrefetchScalarGridSpec(
        num_scalar_prefetch=1,
        grid=(bd, n_pages // pg),
        in_specs=[row(QKW), row(QKW), row(QKW),
                  pl.BlockSpec((None, 1, pg * PAGE), lambda b, s, pt: (b, 0, s)),
                  pl.BlockSpec((None, 1, PAGE), lambda b, s, pt: (b, 0, past // PAGE))]
                 + [page_spec(i) for i in range(pg)] * 2,
        out_specs=row(QKW),
        scratch_shapes=[pltpu.VMEM((16, 1), F32), pltpu.VMEM((16, 1), F32), pltpu.VMEM((16, QKW), F32)],
    )
    return pl.pallas_call(
        functools.partial(_decode_attend_body, pg=pg),
        grid_spec=grid_spec,
        out_shape=jax.ShapeDtypeStruct((bd, 1, QKW), F32),
        compiler_params=_params(("parallel", "arbitrary")),
        name="decode_attend",
    )(page_table, ps["q"].reshape(bd, 1, QKW), ps["k"].reshape(bd, 1, QKW), ps["v"].reshape(bd, 1, QKW),
      sel, sel, *([cache_k] * pg), *([cache_v] * pg))


def _rope_tables(pos):
    inv = ROPE_THETA ** (-jnp.arange(0, HD, 2, dtype=F32) / HD)
    ang = pos.astype(F32)[:, None] * inv[None, :]
    cos, sin = jnp.cos(ang), jnp.sin(ang)
    zero = jnp.zeros_like(sin)
    tile = lambda a, b: jnp.tile(jnp.concatenate([a, b], axis=1), (1, HEADS))
    return tile(cos, cos), tile(-sin, zero), tile(zero, sin)


def _pick_tile(n, pref):
    t = min(n, pref)
    while n % t:
        t //= 2
    return t


def kernel(x_prompt, x_sample, mem_prompt, state_gla, cache_k, cache_v, cache_ik, cache_mem_k, cache_mem_v,
           page_table, norm_in, w_in, w_gla_g2, b_gla_g, gla_onorm, mem_norm, w_mem_kv, w_out, norm_final):
    bp, tp, d = x_prompt.shape
    bs, ts, _ = x_sample.shape
    n_pool = cache_k.shape[0]
    n_pages = page_table.shape[1]
    n_mem = mem_prompt.shape[1]
    assert w_in.shape[0] == 1 and ts == 1, "single layer, single decode token"
    past = n_pages * PAGE

    w = w_in[0]
    sizes = (QKW, QKW, GVW, GATE_RANK, GVW, QKW, QKW, QKW, QKW, HD, HEADS, QKW, QKW, QKW)
    offs = np.cumsum(sizes)[:-1].tolist()
    (w_gq, w_gk, w_gv, w_glr, w_ggate, w_q, w_k, w_v, w_iq, w_ik, w_iw, w_dgate, w_mq, w_mgate) = jnp.split(
        w, offs, axis=1)
    w_tail = jnp.concatenate([w_glr, w_iw, jnp.zeros((d, 128 - GATE_RANK - HEADS), F32)], axis=1)
    w_cat = jnp.concatenate([w_gq, w_gk, w_gv, w_ggate, w_q, w_k, w_v, w_iq, jnp.tile(w_ik, (1, HEADS)),
                             w_dgate, w_mq, w_mgate, w_tail], axis=1).astype(BF16)
    wvt = w_gv.T.astype(BF16)
    w2pad = jnp.zeros((128, QKW), F32).at[:GATE_RANK].set(w_gla_g2[0]).astype(BF16)
    bg = b_gla_g[0].reshape(1, QKW)
    g_in = norm_in[0].reshape(1, d)
    wo = w_out[0].astype(BF16)
    onorm = gla_onorm[0].reshape(1, GLA_DV)
    nf = norm_final.reshape(1, d)

    xp = x_prompt.reshape(bp * tp, d)
    tm = _pick_tile(tp, 256)
    pp = _project(xp, g_in, w_cat, wvt, w2pad, bg, *_rope_tables(jnp.arange(tp)), tm)
    gla_o, st = _gla_prompt(pp, bp, tp, _pick_tile(tp, 128))
    dsa_o = _dsa_prompt(pp, bp, tp, _pick_tile(tp, 256))
    mk, mv = _mem_kv(mem_prompt.reshape(bp * n_mem, d), mem_norm[0].reshape(1, d),
                     w_mem_kv[0].astype(BF16), _pick_tile(n_mem, 256))
    mem_o = _mem_attend_prompt(pp["mq"], mk, mv, bp, tp, n_mem, tm)
    y_prompt = _combine(gla_o, pp["ggate"], dsa_o, pp["dgate"], mem_o, pp["mgate"], xp, onorm, wo, nf, tm)

    xs = x_sample.reshape(bs, d)
    pos_s = jnp.full((bs,), past, I32)
    ps = _project(xs, g_in, w_cat, wvt, w2pad, bg, *_rope_tables(pos_s), bs)
    gla_os, state_new, mem_os = _decode_misc(
        state_gla[0].reshape(bs, QKW, GLA_DV), ps,
        cache_mem_k[0].reshape(bs, n_mem, QKW), cache_mem_v[0].reshape(bs, n_mem, QKW))
    pg = _pick_tile(n_pages, 16)
    sel = _decode_select(ps, cache_ik, page_table, pg)
    dsa_os = _decode_attend(ps, sel, cache_k.reshape(n_pool, 1, PAGE, QKW),
                            cache_v.reshape(n_pool, 1, PAGE, QKW), page_table, pg)
    y_sample = _combine(gla_os.reshape(bs, GVW), ps["ggate"], dsa_os.reshape(bs, QKW), ps["dgate"],
                        mem_os.reshape(bs, QKW), ps["mgate"], xs, onorm, wo, nf, bs)

    st5 = st.reshape(bp, HEADS, GLA_DV, HEADS, HD)
    gla_state_p = jnp.stack([st5[:, h, :, h, :] for h in range(HEADS)], axis=1).swapaxes(-1, -2)
    return (y_prompt.reshape(bp, tp, d), y_sample.reshape(bs, ts, d),
            gla_state_p[None],
            pp["k"].reshape(bp, 1, tp, HEADS, HD), pp["v"].reshape(bp, 1, tp, HEADS, HD),
            pp["ik"].reshape(bp, 1, tp, HD),
            mk.reshape(1, bp, n_mem, HEADS, HD), mv.reshape(1, bp, n_mem, HEADS, HD),
            state_new.reshape(1, bs, HEADS, HD, GLA_DV),
            ps["k"].reshape(bs, 1, ts, HEADS, HD), ps["v"].reshape(bs, 1, ts, HEADS, HD),
            ps["ik"].reshape(bs, 1, ts, HD))
```

```python
import functools

import jax
import jax.numpy as jnp
import numpy as np
from jax import lax
from jax.experimental import pallas as pl
from jax.experimental.pallas import tpu as pltpu

F32 = jnp.float32
BF16 = jnp.bfloat16
I32 = jnp.int32

EPS = 1e-6
ROPE_THETA = 10000.0
HEADS = 4
HD = 64
GLA_DV = 128
QKW = HEADS * HD
GVW = HEADS * GLA_DV
GATE_RANK = 16
GATE_NORM = 16.0
GLA_CHUNK = 64
GLA_SUB = 16
TOPK_MAX = 256
PAGE = 128
NEG = -1e30
INT_MIN = -(2 ** 31)
INT_MAX = 2 ** 31 - 1
VMEM_LIMIT = 56 * 1024 * 1024


def _nn(a, b):
    return jnp.dot(a, b, preferred_element_type=F32)


def _nt(a, b):
    return lax.dot_general(a, b, (((1,), (1,)), ((), ())), preferred_element_type=F32)


def _split3(x):
    hi = x.astype(BF16)
    r1 = x - hi.astype(F32)
    mid = r1.astype(BF16)
    lo = (r1 - mid.astype(F32)).astype(BF16)
    return hi, mid, lo


def _head_masks(width, rows=1):
    lane = lax.broadcasted_iota(I32, (rows, HEADS * width), 1)
    return [jnp.where((lane >= h * width) & (lane < (h + 1) * width), 1.0, 0.0).astype(F32)
            for h in range(HEADS)]


def _stacked_head_mask(rows=16):
    r = lax.broadcasted_iota(I32, (rows, QKW), 0)
    lane = lax.broadcasted_iota(I32, (rows, QKW), 1)
    return jnp.where((lane >= r * HD) & (lane < (r + 1) * HD), 1.0, 0.0).astype(F32)


def _sortable(x):
    u = lax.bitcast_convert_type(x, I32)
    return jnp.where(u >= 0, u, u ^ INT_MAX)


def _rms(x, g):
    return x * lax.rsqrt(jnp.mean(x * x, axis=-1, keepdims=True) + EPS) * g


def _silu(x):
    return x * (1.0 / (1.0 + jnp.exp(-x)))


def _params(sem, vmem=VMEM_LIMIT):
    return pltpu.CompilerParams(dimension_semantics=sem, vmem_limit_bytes=vmem)


_ROW_SEGS = (("gq", QKW), ("gk", QKW), ("gv", GVW), ("ggate", GVW), ("q", QKW), ("k", QKW), ("v", QKW),
             ("iq", QKW), ("ik4", QKW), ("dgate", QKW), ("mq", QKW), ("mgate", QKW), ("tail", 128))
_T_SEGS = (("gvt", GVW), ("kT", QKW), ("vT", QKW), ("ikT", HD), ("tailT", 128))


def _offsets(segs):
    out, o = {}, 0
    for name, width in segs:
        out[name] = (o, width)
        o += width
    return out


_ROW_OFF = _offsets(_ROW_SEGS)
_T_OFF = _offsets(_T_SEGS)
TAIL_IW = GATE_RANK
_ROW_WIDTH = dict(_ROW_SEGS, gg=QKW, ik=HD)
_T_HEIGHT = dict(_T_SEGS)
_T_BY_BATCH = ("kT", "vT", "ikT")


def _proj_body(x_ref, gin_ref, w_ref, wt_ref, w2_ref, bg_ref, cos_ref, sa_ref, sb_ref, cost_ref, sint_ref,
               *out_refs, names):
    out = dict(zip(names, out_refs))
    xn = _rms(x_ref[...], gin_ref[...]).astype(BF16)

    def seg(name):
        off, width = _ROW_OFF[name]
        return _nn(xn, w_ref[:, off:off + width])

    def seg_t(name):
        off, height = _T_OFF[name]
        return _nt(wt_ref[off:off + height, :], xn)

    def rope(z):
        return (z * cos_ref[...] + pltpu.roll(z, QKW - HD // 2, 1) * sa_ref[...]
                + pltpu.roll(z, HD // 2, 1) * sb_ref[...])

    def rope_t(zt):
        c, s = cost_ref[...], sint_ref[...]
        parts = []
        for h in range(zt.shape[0] // HD):
            x1, x2 = zt[h * HD:h * HD + HD // 2], zt[h * HD + HD // 2:(h + 1) * HD]
            parts += [x1 * c - x2 * s, x2 * c + x1 * s]
        return jnp.concatenate(parts, axis=0)

    plain = {"gk": 1.0, "gv": 1.0, "ggate": 1.0, "v": 1.0, "dgate": 1.0, "mgate": 1.0,
             "gq": HD ** -0.5, "mq": HD ** -0.5}
    for name, scale in plain.items():
        if name in out:
            out[name][...] = seg(name) if scale == 1.0 else seg(name) * scale
    if "q" in out:
        out["q"][...] = rope(seg("q")) * (HD ** -0.5)
    if "k" in out:
        out["k"][...] = rope(seg("k"))
    if "iq" in out:
        out["iq"][...] = rope(seg("iq"))
    if "ik4" in out or "ik" in out:
        ik4 = rope(seg("ik4"))
        if "ik4" in out:
            out["ik4"][...] = ik4
        if "ik" in out:
            out["ik"][...] = ik4[:, :HD]
    tail = seg("tail")
    if "tail" in out:
        out["tail"][...] = tail
    pre = _nn(tail.astype(BF16), w2_ref[...]) + bg_ref[...]
    out["gg"][...] = -(jnp.maximum(-pre, 0.0) + jnp.log(1.0 + jnp.exp(-jnp.abs(pre)))) * (1.0 / GATE_NORM)
    if "gvt" in out:
        out["gvt"][...] = seg_t("gvt")
    if "tailT" in out:
        out["tailT"][...] = seg_t("tailT")
    if "kT" in out:
        out["kT"][...] = rope_t(seg_t("kT"))
    if "vT" in out:
        out["vT"][...] = seg_t("vT")
    if "ikT" in out:
        out["ikT"][...] = rope_t(seg_t("ikT"))


def _project(x2d, weights, tables, tm, names, seq):
    n, d = x2d.shape
    nt = seq // tm
    cos, sa, sb, cost, sint = tables
    row = lambda w: pl.BlockSpec((tm, w), lambda i: (i, 0))
    const = lambda a: pl.BlockSpec(a.shape, lambda i: (0,) * a.ndim)
    tab = pl.BlockSpec((tm, QKW), lambda i: (i % nt, 0))
    tab_t = pl.BlockSpec((HD // 2, tm), lambda i: (0, i % nt))
    out_shape, out_specs = [], []
    for name in names:
        if name in _T_BY_BATCH:
            out_shape.append(jax.ShapeDtypeStruct((n // seq, _T_HEIGHT[name], seq), F32))
            out_specs.append(pl.BlockSpec((None, _T_HEIGHT[name], tm), lambda i: (i // nt, 0, i % nt)))
        elif name in _T_HEIGHT:
            out_shape.append(jax.ShapeDtypeStruct((_T_HEIGHT[name], n), F32))
            out_specs.append(pl.BlockSpec((_T_HEIGHT[name], tm), lambda i: (0, i)))
        else:
            out_shape.append(jax.ShapeDtypeStruct((n, _ROW_WIDTH[name]), F32))
            out_specs.append(row(_ROW_WIDTH[name]))
    outs = pl.pallas_call(
        functools.partial(_proj_body, names=names),
        grid=(n // tm,),
        in_specs=[row(d)] + [const(a) for a in weights] + [tab, tab, tab, tab_t, tab_t],
        out_specs=out_specs,
        out_shape=out_shape,
        compiler_params=_params(("parallel",)),
        name="proj",
    )(x2d, *weights, cos, sa, sb, cost, sint)
    return dict(zip(names, outs))


def _gla_body(q_ref, k_ref, g_ref, v_ref, vt_ref, lblk_ref, e_ref, mst_ref, o_ref, st_ref,
              s_ref, kpad, bpad, vpad, *, tg):
    t = pl.program_id(1)
    c_sz, sub = GLA_CHUNK, GLA_SUB

    @pl.when(t == 0)
    def _():
        s_ref[...] = jnp.zeros_like(s_ref)
        kpad[0:sub, :] = jnp.zeros((sub, QKW), F32)
        bpad[0:sub, :] = jnp.zeros((sub, QKW), F32)
        vpad[0:sub, :] = jnp.zeros((sub, GVW), F32)

    lblk = lblk_ref[...]
    b_all = sum(_nn(lblk, term) for term in _split3(g_ref[...]))
    kpad[sub:sub + tg, :] = k_ref[...]
    bpad[sub:sub + tg, :] = b_all
    vpad[sub:sub + tg, :] = v_ref[...]

    hm = _head_masks(HD)
    row = lax.broadcasted_iota(I32, (c_sz, QKW), 0)
    rowmod = row & (sub - 1)
    lane_t = lax.broadcasted_iota(I32, (1, tg), 1)
    e_mat = e_ref[...]
    mst = mst_ref[...]

    for c in range(tg // c_sz):
        r0 = c * c_sz
        q = q_ref[r0:r0 + c_sz, :]
        k = k_ref[r0:r0 + c_sz, :]
        v = v_ref[r0:r0 + c_sz, :]
        b = b_all[r0:r0 + c_sz]
        blast = b[c_sz - 1:c_sz]
        v16 = v.astype(BF16)

        o = _nt((q * jnp.exp(b)).astype(BF16), s_ref[...].astype(BF16))

        blocks = [jnp.zeros((sub, GVW), F32)]
        for i in range(1, c_sz // sub):
            bref = b[i * sub - 1:i * sub]
            qi = q[i * sub:(i + 1) * sub] * jnp.exp(b[i * sub:(i + 1) * sub] - bref)
            kp = jnp.where(row < i * sub, k * jnp.exp(jnp.minimum(bref - b, 0.0)), 0.0)
            qs = jnp.concatenate([qi * hm[h] for h in range(HEADS)], axis=0)
            a = _nt(qs.astype(BF16), kp.astype(BF16))
            r = _nn(a.astype(BF16), v16)
            blocks.append(jnp.concatenate(
                [r[h * sub:(h + 1) * sub, h * GLA_DV:(h + 1) * GLA_DV] for h in range(HEADS)], axis=1))
        o = o + jnp.concatenate(blocks, axis=0)

        for d in range(sub):
            lo = sub + r0 - d
            kd = kpad[lo:lo + c_sz, :]
            bd = bpad[lo:lo + c_sz, :]
            vd = vpad[lo:lo + c_sz, :]
            m = rowmod >= d
            w = jnp.where(m, q * kd * jnp.exp(jnp.where(m, b - bd, 0.0)), 0.0)
            o = o + _nn(w.astype(BF16), e_mat) * vd
        o_ref[r0:r0 + c_sz, :] = o

        in_chunk = jnp.where((lane_t >= r0) & (lane_t < r0 + c_sz), 1.0, 0.0)
        vtm = (vt_ref[...] * in_chunk).astype(BF16)
        kpp = (k_ref[...] * jnp.exp(jnp.minimum(blast - b_all, 0.0))).astype(BF16)
        s_ref[...] = s_ref[...] * jnp.exp(blast) + mst * _nn(vtm, kpp)

    @pl.when(t == pl.num_programs(1) - 1)
    def _():
        st_ref[...] = s_ref[...]


def _gla_prompt(p, batch, seq, tg):
    n = batch * seq
    nt = seq // tg
    r = np.arange(tg)
    lblk = ((r[:, None] // GLA_CHUNK == r[None, :] // GLA_CHUNK) & (r[None, :] <= r[:, None]))
    lblk = jnp.asarray(lblk, BF16)
    e_mat = jnp.asarray(np.arange(QKW)[:, None] // HD == np.arange(GVW)[None, :] // GLA_DV, BF16)
    mst = jnp.asarray(np.arange(GVW)[:, None] // GLA_DV == np.arange(QKW)[None, :] // HD, F32)
    row = lambda w: pl.BlockSpec((tg, w), lambda b, t: (b * nt + t, 0))
    const = lambda a: pl.BlockSpec(a.shape, lambda b, t: (0,) * a.ndim)
    return pl.pallas_call(
        functools.partial(_gla_body, tg=tg),
        grid=(batch, nt),
        in_specs=[row(QKW), row(QKW), row(QKW), row(GVW),
                  pl.BlockSpec((GVW, tg), lambda b, t: (0, b * nt + t)),
                  const(lblk), const(e_mat), const(mst)],
        out_specs=[row(GVW), pl.BlockSpec((None, GVW, QKW), lambda b, t: (b, 0, 0))],
        out_shape=[jax.ShapeDtypeStruct((n, GVW), F32), jax.ShapeDtypeStruct((batch, GVW, QKW), F32)],
        scratch_shapes=[pltpu.VMEM((GVW, QKW), F32), pltpu.VMEM((GLA_SUB + tg, QKW), F32),
                        pltpu.VMEM((GLA_SUB + tg, QKW), F32), pltpu.VMEM((GLA_SUB + tg, GVW), F32)],
        compiler_params=_params(("parallel", "arbitrary")),
        name="gla_prompt",
    )(p["gq"], p["gk"], p["gg"], p["gv"], p["gvt"], lblk, e_mat, mst)


def _ind(cond):
    return jnp.where(cond, 1.0, 0.0)


def _kth_threshold(count, kf):
    c0 = count(lambda key, idx: _ind(key >= 0))
    thr = jnp.where(c0 >= kf, 0, INT_MIN).astype(I32)

    def bit(i, thr):
        cand = thr + jnp.left_shift(jnp.int32(1), 30 - i)
        c = count(lambda key, idx: _ind(key >= cand))
        return jnp.where(c >= kf, cand, thr)

    return lax.fori_loop(0, 31, bit, thr)


def _tie_cutoff(count, thr, need, idx_bits):
    cut = jnp.zeros_like(thr)
    for bit in range(idx_bits - 1, -1, -1):
        cand = cut + (1 << bit)
        c = count(lambda key, idx: jnp.where(key == thr, _ind(idx < cand), 0.0))
        cut = jnp.where(c < need, cand, cut)
    return cut


def _selected(key, idx, thr, cut):
    return jnp.where(key > thr, 1.0, jnp.where(key == thr, _ind(idx <= cut), 0.0))


_CNT_ROWS = 32


def _dsa_body(q_ref, iq_ref, iwt_ref, k_ref, ik4_ref, vt_ref, o_ref,
              keys, thr_ref, cut_ref, m_s, l_s, acc_s, *, tq, topk, idx_bits):
    j = pl.program_id(1)
    nkb = j + 1
    kf = float(topk)
    hm = _head_masks(HD)
    krow = lax.broadcasted_iota(I32, (tq, tq), 0)
    qpos = j * tq + lax.broadcasted_iota(I32, (tq, tq), 1)

    iq = iq_ref[...]
    iqh = [(iq * hm[h]).astype(BF16) for h in range(HEADS)]
    iw = [iwt_ref[TAIL_IW + h:TAIL_IW + h + 1, :] * (HEADS ** -0.5) for h in range(HEADS)]

    def score_block(kb, carry):
        off = pl.multiple_of(kb * tq, tq)
        ikb = ik4_ref[pl.ds(off, tq), :].astype(BF16)
        sc = jnp.zeros((tq, tq), F32)
        for h in range(HEADS):
            sc = sc + jnp.maximum(_nt(ikb, iqh[h]), 0.0) * iw[h]
        sc = jnp.where(kb * tq + krow <= qpos, sc, -jnp.inf)
        keys[pl.ds(off, tq), :] = _sortable(sc)
        return carry

    lax.fori_loop(0, nkb, score_block, 0)

    def count(pred):
        def body(kb, acc):
            off = pl.multiple_of(kb * tq, tq)
            m = pred(keys[pl.ds(off, tq), :], kb * tq + krow)
            return acc + jnp.sum(m.reshape(tq // _CNT_ROWS, _CNT_ROWS, tq), axis=0)
        acc = lax.fori_loop(0, nkb, body, jnp.zeros((_CNT_ROWS, tq), F32))
        return jnp.sum(acc, axis=0, keepdims=True)

    thr_ref[...] = jnp.full((1, tq), INT_MIN, I32)
    cut_ref[...] = jnp.full((1, tq), INT_MAX, I32)

    @pl.when(nkb * tq > topk)
    def _():
        thr = _kth_threshold(count, kf)
        thr_ref[...] = thr
        c_ge = count(lambda key, idx: _ind(key >= thr))

        @pl.when(jnp.max(c_ge) > kf)
        def _():
            need = kf - count(lambda key, idx: _ind(key > thr))
            cut_ref[...] = _tie_cutoff(count, thr, need, idx_bits)

    thr = thr_ref[...]
    cut = cut_ref[...]
    q = q_ref[...]
    qh = [(q * hm[h]).astype(BF16) for h in range(HEADS)]
    m_s[...] = jnp.full(m_s.shape, NEG, F32)
    l_s[...] = jnp.zeros(l_s.shape, F32)
    acc_s[...] = jnp.zeros(acc_s.shape, F32)

    def attend_block(kb, carry):
        off = pl.multiple_of(kb * tq, tq)
        kb16 = k_ref[pl.ds(off, tq), :].astype(BF16)
        idx = kb * tq + krow
        sel = jnp.where(idx <= qpos, _selected(keys[pl.ds(off, tq), :], idx, thr, cut), 0.0)
        bias = jnp.where(sel > 0.5, 0.0, NEG)
        for h in range(HEADS):
            s = _nt(kb16, qh[h]) + bias
            m_old = m_s[h]
            m_new = jnp.maximum(m_old, jnp.max(s, axis=0, keepdims=True))
            a = jnp.exp(m_old - m_new)
            p = jnp.exp(s - m_new)
            l_s[h] = a * l_s[h] + jnp.sum(p, axis=0, keepdims=True)
            m_s[h] = m_new
            rows = slice(h * HD, (h + 1) * HD)
            vth = vt_ref[rows, pl.ds(off, tq)].astype(BF16)
            acc_s[rows, :] = acc_s[rows, :] * a + _nn(vth, p.astype(BF16))
        return carry

    lax.fori_loop(0, nkb, attend_block, 0)
    for h in range(HEADS):
        rows = slice(h * HD, (h + 1) * HD)
        acc_s[rows, :] = acc_s[rows, :] * (1.0 / l_s[h])
    o_ref[...] = acc_s[...].T


def _dsa_prompt(p, batch, seq, tq):
    n = batch * seq
    nq = seq // tq
    topk = min(TOPK_MAX, seq // 4)
    row = lambda w: pl.BlockSpec((tq, w), lambda b, j: (b * nq + j, 0))
    full = lambda w: pl.BlockSpec((seq, w), lambda b, j: (b, 0))
    return pl.pallas_call(
        functools.partial(_dsa_body, tq=tq, topk=topk, idx_bits=max(1, (seq - 1).bit_length())),
        grid=(batch, nq),
        in_specs=[row(QKW), row(QKW), pl.BlockSpec((128, tq), lambda b, j: (0, b * nq + j)),
                  full(QKW), full(QKW), pl.BlockSpec((None, QKW, seq), lambda b, j: (b, 0, 0))],
        out_specs=row(QKW),
        out_shape=jax.ShapeDtypeStruct((n, QKW), F32),
        scratch_shapes=[pltpu.VMEM((seq, tq), I32), pltpu.VMEM((1, tq), I32), pltpu.VMEM((1, tq), I32),
                        pltpu.VMEM((HEADS, 1, tq), F32), pltpu.VMEM((HEADS, 1, tq), F32),
                        pltpu.VMEM((QKW, tq), F32)],
        compiler_params=_params(("parallel", "arbitrary")),
        name="dsa_prompt",
    )(p["q"], p["iq"], p["tailT"], p["k"], p["ik4"], p["vT"])


def _memkv_body(x_ref, g_ref, w_ref, wt_ref, mk_o, mv_o, mkt_o, mvt_o):
    xn = _rms(x_ref[...], g_ref[...]).astype(BF16)
    mk_o[...] = _nn(xn, w_ref[:, :QKW])
    mv_o[...] = _nn(xn, w_ref[:, QKW:])
    mkt_o[...] = _nt(wt_ref[:QKW, :], xn)
    mvt_o[...] = _nt(wt_ref[QKW:, :], xn)


def _mem_kv(mem2d, g, w, wt, batch, n_mem):
    d = mem2d.shape[1]
    const = lambda a: pl.BlockSpec(a.shape, lambda i: (0,) * a.ndim)
    rows = pl.BlockSpec((n_mem, QKW), lambda i: (i, 0))
    tr = pl.BlockSpec((None, QKW, n_mem), lambda i: (i, 0, 0))
    return pl.pallas_call(
        _memkv_body,
        grid=(batch,),
        in_specs=[pl.BlockSpec((n_mem, d), lambda i: (i, 0)), const(g), const(w), const(wt)],
        out_specs=[rows, rows, tr, tr],
        out_shape=[jax.ShapeDtypeStruct((batch * n_mem, QKW), F32)] * 2
                  + [jax.ShapeDtypeStruct((batch, QKW, n_mem), F32)] * 2,
        compiler_params=_params(("parallel",)),
        name="mem_kv",
    )(mem2d, g, w, wt)


def _memattn_body(q_ref, mk_ref, mv_ref, o_ref):
    hm = _head_masks(HD)
    q = q_ref[...]
    mk = mk_ref[...].astype(BF16)
    mv = mv_ref[...]
    ps = []
    inv = jnp.zeros(q.shape, F32)
    for h in range(HEADS):
        s = _nt((q * hm[h]).astype(BF16), mk)
        p = jnp.exp(s - jnp.max(s, axis=-1, keepdims=True))
        inv = inv + hm[h] * (1.0 / jnp.sum(p, axis=-1, keepdims=True))
        ps.append(p.astype(BF16))
    vbd = jnp.concatenate([(mv * hm[h]).astype(BF16) for h in range(HEADS)], axis=0)
    o_ref[...] = _nn(jnp.concatenate(ps, axis=1), vbd) * inv


def _mem_attend_prompt(mq, mk, mv, batch, seq, n_mem, tm):
    nt = seq // tm
    return pl.pallas_call(
        _memattn_body,
        grid=(batch, nt),
        in_specs=[pl.BlockSpec((tm, QKW), lambda b, t: (b * nt + t, 0)),
                  pl.BlockSpec((n_mem, QKW), lambda b, t: (b, 0)),
                  pl.BlockSpec((n_mem, QKW), lambda b, t: (b, 0))],
        out_specs=pl.BlockSpec((tm, QKW), lambda b, t: (b * nt + t, 0)),
        out_shape=jax.ShapeDtypeStruct((batch * seq, QKW), F32),
        compiler_params=_params(("parallel", "parallel")),
        name="mem_attend",
    )(mq, mk, mv)


def _combine_body(gla_ref, ggate_ref, dsa_ref, dgate_ref, mem_ref, mgate_ref, h_ref, on_ref, wo_ref,
                  nf_ref, y_ref):
    gla = gla_ref[...]
    onorm = on_ref[...]
    a = jnp.concatenate([_rms(gla[:, h * GLA_DV:(h + 1) * GLA_DV], onorm) for h in range(HEADS)], axis=1)
    a = (a * _silu(ggate_ref[...])).astype(BF16)
    b = (dsa_ref[...] * _silu(dgate_ref[...])).astype(BF16)
    m = (mem_ref[...] * _silu(mgate_ref[...])).astype(BF16)
    out = (_nn(a, wo_ref[0:GVW, :]) + _nn(b, wo_ref[GVW:GVW + QKW, :])
           + _nn(m, wo_ref[GVW + QKW:GVW + 2 * QKW, :]))
    y_ref[...] = _rms(h_ref[...] + out, nf_ref[...])


def _combine(gla_o, ggate, dsa_o, dgate, mem_o, mgate, h2d, onorm, wo, nf, tm):
    n, d = h2d.shape
    row = lambda w: pl.BlockSpec((tm, w), lambda i: (i, 0))
    const = lambda a: pl.BlockSpec(a.shape, lambda i: (0,) * a.ndim)
    return pl.pallas_call(
        _combine_body,
        grid=(n // tm,),
        in_specs=[row(GVW), row(GVW), row(QKW), row(QKW), row(QKW), row(QKW), row(d),
                  const(onorm), const(wo), const(nf)],
        out_specs=row(d),
        out_shape=jax.ShapeDtypeStruct((n, d), F32),
        compiler_params=_params(("parallel",)),
        name="combine",
    )(gla_o, ggate, dsa_o, dgate, mem_o, mgate, h2d, onorm, wo, nf)


def _decode_misc_body(s_ref, qc_ref, kc_ref, gc_ref, v_ref, mq_ref, mkt_ref, mvt_ref,
                      o_ref, sn_ref, mo_ref):
    s = s_ref[...]
    qc, kc, eg = qc_ref[...], kc_ref[...], jnp.exp(gc_ref[...])
    outs = []
    for h in range(HEADS):
        rows = slice(h * HD, (h + 1) * HD)
        vrow = v_ref[:, h * GLA_DV:(h + 1) * GLA_DV]
        sh = s[rows]
        sn_ref[rows, :] = eg[rows] * sh + kc[rows] * vrow
        o_inter = jnp.sum((qc[rows] * eg[rows]) * sh, axis=0, keepdims=True)
        a = jnp.sum(qc[rows] * kc[rows], axis=0, keepdims=True)
        outs.append(o_inter + a * vrow)
    o_ref[...] = jnp.concatenate(outs, axis=1)

    hm16 = _stacked_head_mask()
    qs = (mq_ref[...] * hm16).astype(BF16)
    sc = _nn(qs, mkt_ref[...].astype(BF16))
    p = jnp.exp(sc - jnp.max(sc, axis=-1, keepdims=True))
    r = _nt(p.astype(BF16), mvt_ref[...].astype(BF16)) / jnp.sum(p, axis=-1, keepdims=True)
    mo_ref[...] = jnp.sum(r * hm16, axis=0, keepdims=True)


def _decode_misc(state, ps, mem_kt, mem_vt):
    bd = state.shape[0]
    n_mem = mem_kt.shape[2]
    col = lambda a: a.reshape(bd, QKW, 1)
    cspec = pl.BlockSpec((None, QKW, 1), lambda b: (b, 0, 0))
    rspec = lambda w: pl.BlockSpec((None, 1, w), lambda b: (b, 0, 0))
    sspec = pl.BlockSpec((None, QKW, GLA_DV), lambda b: (b, 0, 0))
    mspec = pl.BlockSpec((None, QKW, n_mem), lambda b: (b, 0, 0))
    return pl.pallas_call(
        _decode_misc_body,
        grid=(bd,),
        in_specs=[sspec, cspec, cspec, cspec, rspec(GVW), rspec(QKW), mspec, mspec],
        out_specs=[rspec(GVW), sspec, rspec(QKW)],
        out_shape=[jax.ShapeDtypeStruct((bd, 1, GVW), F32), jax.ShapeDtypeStruct((bd, QKW, GLA_DV), F32),
                   jax.ShapeDtypeStruct((bd, 1, QKW), F32)],
        compiler_params=_params(("parallel",)),
        name="decode_gla_mem",
    )(state, col(ps["gq"]), col(ps["gk"]), col(ps["gg"]), ps["gv"].reshape(bd, 1, GVW),
      ps["mq"].reshape(bd, 1, QKW), mem_kt, mem_vt)


def _decode_scores_body(pt_ref, iq_ref, iw_ref, ikn_ref, *rest, pg, past):
    pages, sc_ref = rest[:pg], rest[pg]
    s = pl.program_id(1)
    qs = iq_ref[...]
    qs16 = qs.astype(BF16)
    iw = iw_ref[...]
    rows = [jnp.sum(jnp.maximum(_nn(qs16, pages[i][...].astype(BF16)), 0.0) * iw, axis=0, keepdims=True)
            for i in range(pg)]
    off = pl.multiple_of(s * (pg * PAGE), pg * PAGE)
    sc_ref[:, pl.ds(off, pg * PAGE)] = jnp.concatenate(rows, axis=1)

    @pl.when(s == pl.num_programs(1) - 1)
    def _():
        rn = jnp.maximum(jnp.sum(qs * ikn_ref[...], axis=-1, keepdims=True), 0.0)
        new = jnp.sum(rn * iw, axis=0, keepdims=True)
        lane = lax.broadcasted_iota(I32, (1, PAGE), 1)
        sc_ref[:, past:past + PAGE] = jnp.where(lane == 0, new, -jnp.inf)


def _decode_scores(ps, cache_ikt, page_table, pg):
    bd, n_pages = page_table.shape
    past = n_pages * PAGE
    total = past + PAGE
    pad = lambda a: jnp.pad(a, ((0, 0), (0, 16 - HEADS), (0, 0)))
    iq16 = pad(ps["iq"].reshape(bd, HEADS, HD))
    iw16 = pad((ps["tail"][:, TAIL_IW:TAIL_IW + HEADS] * (HEADS ** -0.5)).reshape(bd, HEADS, 1))
    page_spec = lambda i: pl.BlockSpec((None, None, HD, PAGE),
                                       lambda b, s, pt: (pt[b, s * pg + i], 0, 0, 0))
    grid_spec = pltpu.PrefetchScalarGridSpec(
        num_scalar_prefetch=1,
        grid=(bd, n_pages // pg),
        in_specs=[pl.BlockSpec((None, 16, HD), lambda b, s, pt: (b, 0, 0)),
                  pl.BlockSpec((None, 16, 1), lambda b, s, pt: (b, 0, 0)),
                  pl.BlockSpec((None, 1, HD), lambda b, s, pt: (b, 0, 0))]
                 + [page_spec(i) for i in range(pg)],
        out_specs=pl.BlockSpec((None, 1, total), lambda b, s, pt: (b, 0, 0)),
    )
    return pl.pallas_call(
        functools.partial(_decode_scores_body, pg=pg, past=past),
        grid_spec=grid_spec,
        out_shape=jax.ShapeDtypeStruct((bd, 1, total), F32),
        compiler_params=_params(("parallel", "arbitrary")),
        name="decode_scores",
    )(page_table, iq16, iw16, ps["ik"].reshape(bd, 1, HD), *([cache_ikt] * pg))


def _decode_select_body(sc_ref, sel_ref, *, topk, idx_bits):
    key = _sortable(sc_ref[...])
    idx = lax.broadcasted_iota(I32, key.shape, 1)
    kf = float(topk)

    def count(pred):
        return jnp.sum(pred(key, idx), axis=-1, keepdims=True)

    thr = _kth_threshold(count, kf)
    need = kf - count(lambda kk, ii: _ind(kk > thr))
    cut = _tie_cutoff(count, thr, need, idx_bits)
    sel_ref[...] = _selected(key, idx, thr, cut)


def _decode_select(scores, topk):
    bd, total = scores.shape
    return pl.pallas_call(
        functools.partial(_decode_select_body, topk=topk, idx_bits=max(1, (total - 1).bit_length())),
        out_shape=jax.ShapeDtypeStruct((bd, total), F32),
        compiler_params=pltpu.CompilerParams(vmem_limit_bytes=VMEM_LIMIT),
        name="decode_select",
    )(scores)


def _decode_attend_body(pt_ref, q_ref, kn_ref, vn_ref, sel_ref, seln_ref, *rest, pg):
    kpages, vpages = rest[:pg], rest[pg:2 * pg]
    o_ref, m_s, l_s, acc_s = rest[2 * pg:]
    s = pl.program_id(1)

    @pl.when(s == 0)
    def _():
        m_s[...] = jnp.full(m_s.shape, NEG, F32)
        l_s[...] = jnp.zeros(l_s.shape, F32)
        acc_s[...] = jnp.zeros(acc_s.shape, F32)

    bias = jnp.where(sel_ref[...] > 0.5, 0.0, NEG)
    for h in range(HEADS):
        qc = q_ref[h]
        sc = jnp.concatenate([jnp.sum(qc * kpages[i][h], axis=0, keepdims=True) for i in range(pg)],
                             axis=1) + bias
        m_old = m_s[h]
        m_new = jnp.maximum(m_old, jnp.max(sc, axis=1, keepdims=True))
        a = jnp.exp(m_old - m_new)
        p = jnp.exp(sc - m_new)
        l_s[h] = a * l_s[h] + jnp.sum(p, axis=1, keepdims=True)
        m_s[h] = m_new
        acc = acc_s[h] * a
        for i in range(pg):
            acc = acc + vpages[i][h] * p[:, i * PAGE:(i + 1) * PAGE]
        acc_s[h] = acc

    @pl.when(s == pl.num_programs(1) - 1)
    def _():
        sel_new = seln_ref[:, 0:1] > 0.5
        for h in range(HEADS):
            sn = jnp.where(sel_new, jnp.sum(q_ref[h] * kn_ref[h], axis=0, keepdims=True), NEG)
            m_old = m_s[h]
            m_new = jnp.maximum(m_old, sn)
            a = jnp.exp(m_old - m_new)
            pn = jnp.where(sel_new, jnp.exp(sn - m_new), 0.0)
            tot = a * jnp.sum(acc_s[h], axis=1, keepdims=True) + pn * vn_ref[h]
            o_ref[h] = tot / (a * l_s[h] + pn)


def _decode_attend(ps, sel, cache_kt, cache_vt, page_table, pg):
    bd, n_pages = page_table.shape
    past = n_pages * PAGE
    col = lambda a: a.reshape(bd, HEADS, HD, 1)
    cspec = pl.BlockSpec((None, HEADS, HD, 1), lambda b, s, pt: (b, 0, 0, 0))
    page_spec = lambda i: pl.BlockSpec((None, None, HEADS, HD, PAGE),
                                       lambda b, s, pt: (pt[b, s * pg + i], 0, 0, 0, 0))
    grid_spec = pltpu.PrefetchScalarGridSpec(
        num_scalar_prefetch=1,
        grid=(bd, n_pages // pg),
        in_specs=[cspec, cspec, cspec,
                  pl.BlockSpec((None, 1, pg * PAGE), lambda b, s, pt: (b, 0, s)),
                  pl.BlockSpec((None, 1, PAGE), lambda b, s, pt: (b, 0, past // PAGE))]
                 + [page_spec(i) for i in range(pg)] * 2,
        out_specs=cspec,
        scratch_shapes=[pltpu.VMEM((HEADS, 1, 1), F32), pltpu.VMEM((HEADS, 1, 1), F32),
                        pltpu.VMEM((HEADS, HD, PAGE), F32)],
    )
    sel3 = sel.reshape(bd, 1, past + PAGE)
    return pl.pallas_call(
        functools.partial(_decode_attend_body, pg=pg),
        grid_spec=grid_spec,
        out_shape=jax.ShapeDtypeStruct((bd, HEADS, HD, 1), F32),
        compiler_params=_params(("parallel", "arbitrary")),
        name="decode_attend",
    )(page_table, col(ps["q"]), col(ps["k"]), col(ps["v"]), sel3, sel3,
      *([cache_kt] * pg), *([cache_vt] * pg))


def _rope_tables(pos):
    inv = ROPE_THETA ** (-jnp.arange(0, HD, 2, dtype=F32) / HD)
    ang = pos.astype(F32)[:, None] * inv[None, :]
    cos, sin = jnp.cos(ang), jnp.sin(ang)
    zero = jnp.zeros_like(sin)
    tile = lambda a, b: jnp.tile(jnp.concatenate([a, b], axis=1), (1, HEADS))
    return tile(cos, cos), tile(-sin, zero), tile(zero, sin), cos.T, sin.T


def _pick_tile(n, pref):
    t = min(n, pref)
    while n % t:
        t //= 2
    return t


_PROMPT_OUTS = ("gq", "gk", "gv", "gvt", "gg", "ggate", "q", "k", "iq", "ik4", "kT", "vT", "ikT", "tailT",
                "dgate", "mq", "mgate")
_SAMPLE_OUTS = ("gq", "gk", "gv", "gg", "ggate", "q", "k", "v", "iq", "ik", "tail", "dgate", "mq", "mgate")


def kernel(x_prompt, x_sample, mem_prompt, state_gla, cache_k, cache_v, cache_ik, cache_mem_k, cache_mem_v,
           page_table, norm_in, w_in, w_gla_g2, b_gla_g, gla_onorm, mem_norm, w_mem_kv, w_out, norm_final):
    bp, tp, d = x_prompt.shape
    bs, ts, _ = x_sample.shape
    n_pages = page_table.shape[1]
    n_mem = mem_prompt.shape[1]
    assert w_in.shape[0] == 1 and ts == 1, "single layer, single decode token"
    past = n_pages * PAGE

    sizes = (QKW, QKW, GVW, GATE_RANK, GVW, QKW, QKW, QKW, QKW, HD, HEADS, QKW, QKW, QKW)
    offs = np.cumsum(sizes)[:-1].tolist()
    (w_gq, w_gk, w_gv, w_glr, w_ggate, w_q, w_k, w_v, w_iq, w_ik, w_iw, w_dgate, w_mq, w_mgate) = jnp.split(
        w_in[0], offs, axis=1)
    w_tail = jnp.concatenate([w_glr, w_iw, jnp.zeros((d, 128 - GATE_RANK - HEADS), F32)], axis=1)
    w_cat = jnp.concatenate([w_gq, w_gk, w_gv, w_ggate, w_q, w_k, w_v, w_iq, jnp.tile(w_ik, (1, HEADS)),
                             w_dgate, w_mq, w_mgate, w_tail], axis=1).astype(BF16)
    wt_cat = jnp.concatenate([w_gv, w_k, w_v, w_ik, w_tail], axis=1).T.astype(BF16)
    w2pad = jnp.zeros((128, QKW), F32).at[:GATE_RANK].set(w_gla_g2[0]).astype(BF16)
    weights = (norm_in[0].reshape(1, d), w_cat, wt_cat, w2pad, b_gla_g[0].reshape(1, QKW))
    wo = w_out[0].astype(BF16)
    onorm = gla_onorm[0].reshape(1, GLA_DV)
    nf = norm_final.reshape(1, d)

    xp = x_prompt.reshape(bp * tp, d)
    tm = _pick_tile(tp, 256)
    pp = _project(xp, weights, _rope_tables(jnp.arange(tp)), tm, _PROMPT_OUTS, tp)
    gla_o, st = _gla_prompt(pp, bp, tp, _pick_tile(tp, 128))
    dsa_o = _dsa_prompt(pp, bp, tp, _pick_tile(tp, 256))
    wm = w_mem_kv[0].astype(BF16)
    mk, mv, mkt, mvt = _mem_kv(mem_prompt.reshape(bp * n_mem, d), mem_norm[0].reshape(1, d), wm, wm.T, bp, n_mem)
    mem_o = _mem_attend_prompt(pp["mq"], mk, mv, bp, tp, n_mem, tm)
    y_prompt = _combine(gla_o, pp["ggate"], dsa_o, pp["dgate"], mem_o, pp["mgate"], xp, onorm, wo, nf, tm)

    xs = x_sample.reshape(bs, d)
    ps = _project(xs, weights, _rope_tables(jnp.full((bs,), past, I32)), bs, _SAMPLE_OUTS, bs)
    mem_t = lambda c: c[0].transpose(0, 2, 3, 1).reshape(bs, QKW, n_mem)
    gla_os, state_new, mem_os = _decode_misc(state_gla[0].reshape(bs, QKW, GLA_DV), ps,
                                             mem_t(cache_mem_k), mem_t(cache_mem_v))
    pg = _pick_tile(n_pages, 16)
    scores = _decode_scores(ps, cache_ik.transpose(0, 1, 3, 2), page_table, pg)
    sel = _decode_select(scores.reshape(bs, past + PAGE), min(TOPK_MAX, (past + ts) // 4))
    dsa_os = _decode_attend(ps, sel, cache_k.transpose(0, 1, 3, 4, 2), cache_v.transpose(0, 1, 3, 4, 2),
                            page_table, pg)
    y_sample = _combine(gla_os.reshape(bs, GVW), ps["ggate"], dsa_os.reshape(bs, QKW), ps["dgate"],
                        mem_os.reshape(bs, QKW), ps["mgate"], xs, onorm, wo, nf, bs)

    st5 = st.reshape(bp, HEADS, GLA_DV, HEADS, HD)
    gla_state_p = jnp.stack([st5[:, h, :, h, :] for h in range(HEADS)], axis=1).swapaxes(-1, -2)
    heads_t = lambda a, b, t: a.reshape(b, 1, HEADS, HD, t).transpose(0, 1, 4, 2, 3)
    return (y_prompt.reshape(bp, tp, d), y_sample.reshape(bs, ts, d),
            gla_state_p[None],
            heads_t(pp["kT"], bp, tp), heads_t(pp["vT"], bp, tp),
            pp["ikT"].reshape(bp, 1, HD, tp).transpose(0, 1, 3, 2),
            heads_t(mkt, bp, n_mem).swapaxes(0, 1), heads_t(mvt, bp, n_mem).swapaxes(0, 1),
            state_new.reshape(1, bs, HEADS, HD, GLA_DV),
            ps["k"].reshape(bs, 1, ts, HEADS, HD), ps["v"].reshape(bs, 1, ts, HEADS, HD),
            ps["ik"].reshape(bs, 1, ts, HD))
```

```python
import functools

import jax
import jax.numpy as jnp
import numpy as np
from jax import lax
from jax.experimental import pallas as pl
from jax.experimental.pallas import tpu as pltpu

F32 = jnp.float32
BF16 = jnp.bfloat16
I32 = jnp.int32

EPS = 1e-6
ROPE_THETA = 10000.0
HEADS = 4
HD = 64
GLA_DV = 128
QKW = HEADS * HD
GVW = HEADS * GLA_DV
GATE_RANK = 16
GATE_NORM = 16.0
GLA_CHUNK = 64
GLA_SUB = 8
TOPK_MAX = 256
PAGE = 128
NEG = -1e30
INT_MIN = -(2 ** 31)
INT_MAX = 2 ** 31 - 1
VMEM_LIMIT = 56 * 1024 * 1024


def _nn(a, b):
    return jnp.dot(a, b, preferred_element_type=F32)


def _nt(a, b):
    return lax.dot_general(a, b, (((1,), (1,)), ((), ())), preferred_element_type=F32)


def _split3(x):
    hi = x.astype(BF16)
    r1 = x - hi.astype(F32)
    mid = r1.astype(BF16)
    lo = (r1 - mid.astype(F32)).astype(BF16)
    return hi, mid, lo


def _head_masks(width, rows=1):
    lane = lax.broadcasted_iota(I32, (rows, HEADS * width), 1)
    return [jnp.where((lane >= h * width) & (lane < (h + 1) * width), 1.0, 0.0).astype(F32)
            for h in range(HEADS)]


def _stacked_head_mask(rows=16):
    r = lax.broadcasted_iota(I32, (rows, QKW), 0)
    lane = lax.broadcasted_iota(I32, (rows, QKW), 1)
    return jnp.where((lane >= r * HD) & (lane < (r + 1) * HD), 1.0, 0.0).astype(F32)


def _sortable(x):
    u = lax.bitcast_convert_type(x, I32)
    return jnp.where(u >= 0, u, u ^ INT_MAX)


def _rms(x, g):
    return x * lax.rsqrt(jnp.mean(x * x, axis=-1, keepdims=True) + EPS) * g


def _silu(x):
    return x * (1.0 / (1.0 + jnp.exp(-x)))


def _params(sem, vmem=VMEM_LIMIT):
    return pltpu.CompilerParams(dimension_semantics=sem, vmem_limit_bytes=vmem)


_ROW_SEGS = (("gq", QKW), ("gk", QKW), ("gv", GVW), ("ggate", GVW), ("q", QKW), ("k", QKW), ("v", QKW),
             ("iq", QKW), ("ik4", QKW), ("dgate", QKW), ("mq", QKW), ("mgate", QKW), ("tail", 128))
_T_SEGS = (("gvt", GVW), ("kT", QKW), ("vT", QKW), ("ikT", HD), ("tailT", 128))


def _offsets(segs):
    out, o = {}, 0
    for name, width in segs:
        out[name] = (o, width)
        o += width
    return out


_ROW_OFF = _offsets(_ROW_SEGS)
_T_OFF = _offsets(_T_SEGS)
TAIL_IW = GATE_RANK
_ROW_WIDTH = dict(_ROW_SEGS, gg=QKW, ik=HD)
_T_HEIGHT = dict(_T_SEGS)
_T_BY_BATCH = ("kT", "vT", "ikT")


def _proj_body(x_ref, gin_ref, w_ref, wt_ref, w2_ref, bg_ref, cos_ref, sa_ref, sb_ref, cost_ref, sint_ref,
               *out_refs, names):
    out = dict(zip(names, out_refs))
    xn = _rms(x_ref[...], gin_ref[...]).astype(BF16)

    def seg(name):
        off, width = _ROW_OFF[name]
        return _nn(xn, w_ref[:, off:off + width])

    def seg_t(name):
        off, height = _T_OFF[name]
        return _nt(wt_ref[off:off + height, :], xn)

    def rope(z):
        return (z * cos_ref[...] + pltpu.roll(z, QKW - HD // 2, 1) * sa_ref[...]
                + pltpu.roll(z, HD // 2, 1) * sb_ref[...])

    def rope_t(zt):
        c, s = cost_ref[...], sint_ref[...]
        parts = []
        for h in range(zt.shape[0] // HD):
            x1, x2 = zt[h * HD:h * HD + HD // 2], zt[h * HD + HD // 2:(h + 1) * HD]
            parts += [x1 * c - x2 * s, x2 * c + x1 * s]
        return jnp.concatenate(parts, axis=0)

    plain = {"gk": 1.0, "gv": 1.0, "ggate": 1.0, "v": 1.0, "dgate": 1.0, "mgate": 1.0,
             "gq": HD ** -0.5, "mq": HD ** -0.5}
    for name, scale in plain.items():
        if name in out:
            out[name][...] = seg(name) if scale == 1.0 else seg(name) * scale
    if "q" in out:
        out["q"][...] = rope(seg("q")) * (HD ** -0.5)
    if "k" in out:
        out["k"][...] = rope(seg("k"))
    if "iq" in out:
        out["iq"][...] = rope(seg("iq"))
    if "ik4" in out or "ik" in out:
        ik4 = rope(seg("ik4"))
        if "ik4" in out:
            out["ik4"][...] = ik4
        if "ik" in out:
            out["ik"][...] = ik4[:, :HD]
    tail = seg("tail")
    if "tail" in out:
        out["tail"][...] = tail
    pre = _nn(tail.astype(BF16), w2_ref[...]) + bg_ref[...]
    out["gg"][...] = -(jnp.maximum(-pre, 0.0) + jnp.log(1.0 + jnp.exp(-jnp.abs(pre)))) * (1.0 / GATE_NORM)
    if "gvt" in out:
        out["gvt"][...] = seg_t("gvt")
    if "tailT" in out:
        out["tailT"][...] = seg_t("tailT")
    if "kT" in out:
        out["kT"][...] = rope_t(seg_t("kT"))
    if "vT" in out:
        out["vT"][...] = seg_t("vT")
    if "ikT" in out:
        out["ikT"][...] = rope_t(seg_t("ikT"))


def _project(x2d, weights, tables, tm, names, seq):
    n, d = x2d.shape
    nt = seq // tm
    cos, sa, sb, cost, sint = tables
    row = lambda w: pl.BlockSpec((tm, w), lambda i: (i, 0))
    const = lambda a: pl.BlockSpec(a.shape, lambda i: (0,) * a.ndim)
    tab = pl.BlockSpec((tm, QKW), lambda i: (i % nt, 0))
    tab_t = pl.BlockSpec((HD // 2, tm), lambda i: (0, i % nt))
    out_shape, out_specs = [], []
    for name in names:
        if name in _T_BY_BATCH:
            out_shape.append(jax.ShapeDtypeStruct((n // seq, _T_HEIGHT[name], seq), F32))
            out_specs.append(pl.BlockSpec((None, _T_HEIGHT[name], tm), lambda i: (i // nt, 0, i % nt)))
        elif name in _T_HEIGHT:
            out_shape.append(jax.ShapeDtypeStruct((_T_HEIGHT[name], n), F32))
            out_specs.append(pl.BlockSpec((_T_HEIGHT[name], tm), lambda i: (0, i)))
        else:
            out_shape.append(jax.ShapeDtypeStruct((n, _ROW_WIDTH[name]), F32))
            out_specs.append(row(_ROW_WIDTH[name]))
    outs = pl.pallas_call(
        functools.partial(_proj_body, names=names),
        grid=(n // tm,),
        in_specs=[row(d)] + [const(a) for a in weights] + [tab, tab, tab, tab_t, tab_t],
        out_specs=out_specs,
        out_shape=out_shape,
        compiler_params=_params(("parallel",)),
        name="proj",
    )(x2d, *weights, cos, sa, sb, cost, sint)
    return dict(zip(names, outs))


def _gla_body(q_ref, k_ref, g_ref, v_ref, vt_ref, lblk_ref, e_ref, mst_ref, o_ref, st_ref,
              s_ref, kpad, bpad, vpad, *, tg):
    t = pl.program_id(1)
    c_sz, sub = GLA_CHUNK, GLA_SUB

    @pl.when(t == 0)
    def _():
        s_ref[...] = jnp.zeros_like(s_ref)
        kpad[0:sub, :] = jnp.zeros((sub, QKW), F32)
        bpad[0:sub, :] = jnp.zeros((sub, QKW), F32)
        vpad[0:sub, :] = jnp.zeros((sub, GVW), F32)

    lblk = lblk_ref[...]
    b_all = sum(_nn(lblk, term) for term in _split3(g_ref[...]))
    kpad[sub:sub + tg, :] = k_ref[...]
    bpad[sub:sub + tg, :] = b_all
    vpad[sub:sub + tg, :] = v_ref[...]

    hm = _head_masks(HD)
    row = lax.broadcasted_iota(I32, (c_sz, QKW), 0)
    rowmod = row & (sub - 1)
    lane_t = lax.broadcasted_iota(I32, (1, tg), 1)
    e_mat = e_ref[...]
    mst = mst_ref[...]

    for c in range(tg // c_sz):
        r0 = c * c_sz
        q = q_ref[r0:r0 + c_sz, :]
        k = k_ref[r0:r0 + c_sz, :]
        v = v_ref[r0:r0 + c_sz, :]
        b = b_all[r0:r0 + c_sz]
        blast = b[c_sz - 1:c_sz]
        v16 = v.astype(BF16)

        o = _nt((q * jnp.exp(b)).astype(BF16), s_ref[...].astype(BF16))

        blocks = [jnp.zeros((sub, GVW), F32)]
        for i in range(1, c_sz // sub):
            bref = b[i * sub - 1:i * sub]
            qi = q[i * sub:(i + 1) * sub] * jnp.exp(b[i * sub:(i + 1) * sub] - bref)
            kp = jnp.where(row < i * sub, k * jnp.exp(jnp.minimum(bref - b, 0.0)), 0.0)
            qs = jnp.concatenate([qi * hm[h] for h in range(HEADS)], axis=0)
            a = _nt(qs.astype(BF16), kp.astype(BF16))
            r = _nn(a.astype(BF16), v16)
            blocks.append(jnp.concatenate(
                [r[h * sub:(h + 1) * sub, h * GLA_DV:(h + 1) * GLA_DV] for h in range(HEADS)], axis=1))
        o = o + jnp.concatenate(blocks, axis=0)

        for d in range(sub):
            lo = sub + r0 - d
            kd = kpad[lo:lo + c_sz, :]
            bd = bpad[lo:lo + c_sz, :]
            vd = vpad[lo:lo + c_sz, :]
            m = rowmod >= d
            w = jnp.where(m, q * kd * jnp.exp(jnp.where(m, b - bd, 0.0)), 0.0)
            o = o + _nn(w.astype(BF16), e_mat) * vd
        o_ref[r0:r0 + c_sz, :] = o

        in_chunk = jnp.where((lane_t >= r0) & (lane_t < r0 + c_sz), 1.0, 0.0)
        vtm = (vt_ref[...] * in_chunk).astype(BF16)
        kpp = (k_ref[...] * jnp.exp(jnp.minimum(blast - b_all, 0.0))).astype(BF16)
        s_ref[...] = s_ref[...] * jnp.exp(blast) + mst * _nn(vtm, kpp)

    @pl.when(t == pl.num_programs(1) - 1)
    def _():
        st_ref[...] = s_ref[...]


def _gla_prompt(p, batch, seq, tg):
    n = batch * seq
    nt = seq // tg
    r = np.arange(tg)
    lblk = ((r[:, None] // GLA_CHUNK == r[None, :] // GLA_CHUNK) & (r[None, :] <= r[:, None]))
    lblk = jnp.asarray(lblk, BF16)
    e_mat = jnp.asarray(np.arange(QKW)[:, None] // HD == np.arange(GVW)[None, :] // GLA_DV, BF16)
    mst = jnp.asarray(np.arange(GVW)[:, None] // GLA_DV == np.arange(QKW)[None, :] // HD, F32)
    row = lambda w: pl.BlockSpec((tg, w), lambda b, t: (b * nt + t, 0))
    const = lambda a: pl.BlockSpec(a.shape, lambda b, t: (0,) * a.ndim)
    return pl.pallas_call(
        functools.partial(_gla_body, tg=tg),
        grid=(batch, nt),
        in_specs=[row(QKW), row(QKW), row(QKW), row(GVW),
                  pl.BlockSpec((GVW, tg), lambda b, t: (0, b * nt + t)),
                  const(lblk), const(e_mat), const(mst)],
        out_specs=[row(GVW), pl.BlockSpec((None, GVW, QKW), lambda b, t: (b, 0, 0))],
        out_shape=[jax.ShapeDtypeStruct((n, GVW), F32), jax.ShapeDtypeStruct((batch, GVW, QKW), F32)],
        scratch_shapes=[pltpu.VMEM((GVW, QKW), F32), pltpu.VMEM((GLA_SUB + tg, QKW), F32),
                        pltpu.VMEM((GLA_SUB + tg, QKW), F32), pltpu.VMEM((GLA_SUB + tg, GVW), F32)],
        compiler_params=_params(("parallel", "arbitrary")),
        name="gla_prompt",
    )(p["gq"], p["gk"], p["gg"], p["gv"], p["gvt"], lblk, e_mat, mst)


def _ind(cond):
    return jnp.where(cond, 1.0, 0.0)


def _kth_threshold(count, kf):
    c0 = count(lambda key, idx: _ind(key >= 0))
    thr = jnp.where(c0 >= kf, 0, INT_MIN).astype(I32)

    def bit(i, thr):
        cand = thr + jnp.left_shift(jnp.int32(1), 30 - i)
        c = count(lambda key, idx: _ind(key >= cand))
        return jnp.where(c >= kf, cand, thr)

    return lax.fori_loop(0, 31, bit, thr)


def _tie_cutoff(count, thr, need, idx_bits):
    cut = jnp.zeros_like(thr)
    for bit in range(idx_bits - 1, -1, -1):
        cand = cut + (1 << bit)
        c = count(lambda key, idx: jnp.where(key == thr, _ind(idx < cand), 0.0))
        cut = jnp.where(c < need, cand, cut)
    return cut


def _selected(key, idx, thr, cut):
    return jnp.where(key > thr, 1.0, jnp.where(key == thr, _ind(idx <= cut), 0.0))


_CNT_ROWS = 32


def _dsa_body(q_ref, iq_ref, iwt_ref, k_ref, ik4_ref, vt_ref, o_ref,
              keys, thr_ref, cut_ref, m_s, l_s, acc_s, *, tq, topk, idx_bits):
    j = pl.program_id(1)
    nkb = j + 1
    kf = float(topk)
    hm = _head_masks(HD)
    krow = lax.broadcasted_iota(I32, (tq, tq), 0)
    qpos = j * tq + lax.broadcasted_iota(I32, (tq, tq), 1)

    iq = iq_ref[...]
    iqh = [(iq * hm[h]).astype(BF16) for h in range(HEADS)]
    iw = [iwt_ref[TAIL_IW + h:TAIL_IW + h + 1, :] * (HEADS ** -0.5) for h in range(HEADS)]

    def score_block(kb, carry):
        off = pl.multiple_of(kb * tq, tq)
        ikb = ik4_ref[pl.ds(off, tq), :].astype(BF16)
        sc = jnp.zeros((tq, tq), F32)
        for h in range(HEADS):
            sc = sc + jnp.maximum(_nt(ikb, iqh[h]), 0.0) * iw[h]
        sc = jnp.where(kb * tq + krow <= qpos, sc, -jnp.inf)
        keys[pl.ds(off, tq), :] = _sortable(sc)
        return carry

    lax.fori_loop(0, nkb, score_block, 0)

    def count(pred):
        def body(kb, acc):
            off = pl.multiple_of(kb * tq, tq)
            m = pred(keys[pl.ds(off, tq), :], kb * tq + krow)
            return acc + jnp.sum(m.reshape(tq // _CNT_ROWS, _CNT_ROWS, tq), axis=0)
        acc = lax.fori_loop(0, nkb, body, jnp.zeros((_CNT_ROWS, tq), F32))
        return jnp.sum(acc, axis=0, keepdims=True)

    thr_ref[...] = jnp.full((1, tq), INT_MIN, I32)
    cut_ref[...] = jnp.full((1, tq), INT_MAX, I32)

    @pl.when(nkb * tq > topk)
    def _():
        thr = _kth_threshold(count, kf)
        thr_ref[...] = thr
        c_ge = count(lambda key, idx: _ind(key >= thr))

        @pl.when(jnp.max(c_ge) > kf)
        def _():
            need = kf - count(lambda key, idx: _ind(key > thr))
            cut_ref[...] = _tie_cutoff(count, thr, need, idx_bits)

    thr = thr_ref[...]
    cut = cut_ref[...]
    q = q_ref[...]
    qh = [(q * hm[h]).astype(BF16) for h in range(HEADS)]
    m_s[...] = jnp.full(m_s.shape, NEG, F32)
    l_s[...] = jnp.zeros(l_s.shape, F32)
    acc_s[...] = jnp.zeros(acc_s.shape, F32)

    def attend_block(kb, carry):
        off = pl.multiple_of(kb * tq, tq)
        kb16 = k_ref[pl.ds(off, tq), :].astype(BF16)
        idx = kb * tq + krow
        sel = jnp.where(idx <= qpos, _selected(keys[pl.ds(off, tq), :], idx, thr, cut), 0.0)
        bias = jnp.where(sel > 0.5, 0.0, NEG)
        for h in range(HEADS):
            s = _nt(kb16, qh[h]) + bias
            m_old = m_s[h]
            m_new = jnp.maximum(m_old, jnp.max(s, axis=0, keepdims=True))
            a = jnp.exp(m_old - m_new)
            p = jnp.exp(s - m_new)
            l_s[h] = a * l_s[h] + jnp.sum(p, axis=0, keepdims=True)
            m_s[h] = m_new
            rows = slice(h * HD, (h + 1) * HD)
            vth = vt_ref[rows, pl.ds(off, tq)].astype(BF16)
            acc_s[rows, :] = acc_s[rows, :] * a + _nn(vth, p.astype(BF16))
        return carry

    lax.fori_loop(0, nkb, attend_block, 0)
    for h in range(HEADS):
        rows = slice(h * HD, (h + 1) * HD)
        acc_s[rows, :] = acc_s[rows, :] * (1.0 / l_s[h])
    o_ref[...] = acc_s[...].T


def _dsa_prompt(p, batch, seq, tq):
    n = batch * seq
    nq = seq // tq
    topk = min(TOPK_MAX, seq // 4)
    row = lambda w: pl.BlockSpec((tq, w), lambda b, j: (b * nq + j, 0))
    full = lambda w: pl.BlockSpec((seq, w), lambda b, j: (b, 0))
    return pl.pallas_call(
        functools.partial(_dsa_body, tq=tq, topk=topk, idx_bits=max(1, (seq - 1).bit_length())),
        grid=(batch, nq),
        in_specs=[row(QKW), row(QKW), pl.BlockSpec((128, tq), lambda b, j: (0, b * nq + j)),
                  full(QKW), full(QKW), pl.BlockSpec((None, QKW, seq), lambda b, j: (b, 0, 0))],
        out_specs=row(QKW),
        out_shape=jax.ShapeDtypeStruct((n, QKW), F32),
        scratch_shapes=[pltpu.VMEM((seq, tq), I32), pltpu.VMEM((1, tq), I32), pltpu.VMEM((1, tq), I32),
                        pltpu.VMEM((HEADS, 1, tq), F32), pltpu.VMEM((HEADS, 1, tq), F32),
                        pltpu.VMEM((QKW, tq), F32)],
        compiler_params=_params(("parallel", "arbitrary")),
        name="dsa_prompt",
    )(p["q"], p["iq"], p["tailT"], p["k"], p["ik4"], p["vT"])


def _memkv_body(x_ref, g_ref, w_ref, wt_ref, mk_o, mv_o, mkt_o, mvt_o):
    xn = _rms(x_ref[...], g_ref[...]).astype(BF16)
    mk_o[...] = _nn(xn, w_ref[:, :QKW])
    mv_o[...] = _nn(xn, w_ref[:, QKW:])
    mkt_o[...] = _nt(wt_ref[:QKW, :], xn)
    mvt_o[...] = _nt(wt_ref[QKW:, :], xn)


def _mem_kv(mem2d, g, w, wt, batch, n_mem):
    d = mem2d.shape[1]
    const = lambda a: pl.BlockSpec(a.shape, lambda i: (0,) * a.ndim)
    rows = pl.BlockSpec((n_mem, QKW), lambda i: (i, 0))
    tr = pl.BlockSpec((None, QKW, n_mem), lambda i: (i, 0, 0))
    return pl.pallas_call(
        _memkv_body,
        grid=(batch,),
        in_specs=[pl.BlockSpec((n_mem, d), lambda i: (i, 0)), const(g), const(w), const(wt)],
        out_specs=[rows, rows, tr, tr],
        out_shape=[jax.ShapeDtypeStruct((batch * n_mem, QKW), F32)] * 2
                  + [jax.ShapeDtypeStruct((batch, QKW, n_mem), F32)] * 2,
        compiler_params=_params(("parallel",)),
        name="mem_kv",
    )(mem2d, g, w, wt)


def _memattn_body(q_ref, mk_ref, mv_ref, o_ref):
    hm = _head_masks(HD)
    q = q_ref[...]
    mk = mk_ref[...].astype(BF16)
    mv = mv_ref[...]
    ps = []
    inv = jnp.zeros(q.shape, F32)
    for h in range(HEADS):
        s = _nt((q * hm[h]).astype(BF16), mk)
        p = jnp.exp(s - jnp.max(s, axis=-1, keepdims=True))
        inv = inv + hm[h] * (1.0 / jnp.sum(p, axis=-1, keepdims=True))
        ps.append(p.astype(BF16))
    vbd = jnp.concatenate([(mv * hm[h]).astype(BF16) for h in range(HEADS)], axis=0)
    o_ref[...] = _nn(jnp.concatenate(ps, axis=1), vbd) * inv


def _mem_attend_prompt(mq, mk, mv, batch, seq, n_mem, tm):
    nt = seq // tm
    return pl.pallas_call(
        _memattn_body,
        grid=(batch, nt),
        in_specs=[pl.BlockSpec((tm, QKW), lambda b, t: (b * nt + t, 0)),
                  pl.BlockSpec((n_mem, QKW), lambda b, t: (b, 0)),
                  pl.BlockSpec((n_mem, QKW), lambda b, t: (b, 0))],
        out_specs=pl.BlockSpec((tm, QKW), lambda b, t: (b * nt + t, 0)),
        out_shape=jax.ShapeDtypeStruct((batch * seq, QKW), F32),
        compiler_params=_params(("parallel", "parallel")),
        name="mem_attend",
    )(mq, mk, mv)


def _combine_body(gla_ref, ggate_ref, dsa_ref, dgate_ref, mem_ref, mgate_ref, h_ref, on_ref, wo_ref,
                  nf_ref, y_ref):
    gla = gla_ref[...]
    onorm = on_ref[...]
    a = jnp.concatenate([_rms(gla[:, h * GLA_DV:(h + 1) * GLA_DV], onorm) for h in range(HEADS)], axis=1)
    a = (a * _silu(ggate_ref[...])).astype(BF16)
    b = (dsa_ref[...] * _silu(dgate_ref[...])).astype(BF16)
    m = (mem_ref[...] * _silu(mgate_ref[...])).astype(BF16)
    out = (_nn(a, wo_ref[0:GVW, :]) + _nn(b, wo_ref[GVW:GVW + QKW, :])
           + _nn(m, wo_ref[GVW + QKW:GVW + 2 * QKW, :]))
    y_ref[...] = _rms(h_ref[...] + out, nf_ref[...])


def _combine(gla_o, ggate, dsa_o, dgate, mem_o, mgate, h2d, onorm, wo, nf, tm):
    n, d = h2d.shape
    row = lambda w: pl.BlockSpec((tm, w), lambda i: (i, 0))
    const = lambda a: pl.BlockSpec(a.shape, lambda i: (0,) * a.ndim)
    return pl.pallas_call(
        _combine_body,
        grid=(n // tm,),
        in_specs=[row(GVW), row(GVW), row(QKW), row(QKW), row(QKW), row(QKW), row(d),
                  const(onorm), const(wo), const(nf)],
        out_specs=row(d),
        out_shape=jax.ShapeDtypeStruct((n, d), F32),
        compiler_params=_params(("parallel",)),
        name="combine",
    )(gla_o, ggate, dsa_o, dgate, mem_o, mgate, h2d, onorm, wo, nf)


def _decode_misc_body(s_ref, qc_ref, kc_ref, gc_ref, v_ref, mq_ref, mkt_ref, mvt_ref,
                      o_ref, sn_ref, mo_ref):
    s = s_ref[...]
    qc, kc, eg = qc_ref[...], kc_ref[...], jnp.exp(gc_ref[...])
    outs = []
    for h in range(HEADS):
        rows = slice(h * HD, (h + 1) * HD)
        vrow = v_ref[:, h * GLA_DV:(h + 1) * GLA_DV]
        sh = s[rows]
        sn_ref[rows, :] = eg[rows] * sh + kc[rows] * vrow
        o_inter = jnp.sum((qc[rows] * eg[rows]) * sh, axis=0, keepdims=True)
        a = jnp.sum(qc[rows] * kc[rows], axis=0, keepdims=True)
        outs.append(o_inter + a * vrow)
    o_ref[...] = jnp.concatenate(outs, axis=1)

    hm16 = _stacked_head_mask()
    qs = (mq_ref[...] * hm16).astype(BF16)
    sc = _nn(qs, mkt_ref[...].astype(BF16))
    p = jnp.exp(sc - jnp.max(sc, axis=-1, keepdims=True))
    r = _nt(p.astype(BF16), mvt_ref[...].astype(BF16)) / jnp.sum(p, axis=-1, keepdims=True)
    mo_ref[...] = jnp.sum(r * hm16, axis=0, keepdims=True)


def _decode_misc(state, ps, mem_kt, mem_vt):
    bd = state.shape[0]
    n_mem = mem_kt.shape[2]
    col = lambda a: a.reshape(bd, QKW, 1)
    cspec = pl.BlockSpec((None, QKW, 1), lambda b: (b, 0, 0))
    rspec = lambda w: pl.BlockSpec((None, 1, w), lambda b: (b, 0, 0))
    sspec = pl.BlockSpec((None, QKW, GLA_DV), lambda b: (b, 0, 0))
    mspec = pl.BlockSpec((None, QKW, n_mem), lambda b: (b, 0, 0))
    return pl.pallas_call(
        _decode_misc_body,
        grid=(bd,),
        in_specs=[sspec, cspec, cspec, cspec, rspec(GVW), rspec(QKW), mspec, mspec],
        out_specs=[rspec(GVW), sspec, rspec(QKW)],
        out_shape=[jax.ShapeDtypeStruct((bd, 1, GVW), F32), jax.ShapeDtypeStruct((bd, QKW, GLA_DV), F32),
                   jax.ShapeDtypeStruct((bd, 1, QKW), F32)],
        compiler_params=_params(("parallel",)),
        name="decode_gla_mem",
    )(state, col(ps["gq"]), col(ps["gk"]), col(ps["gg"]), ps["gv"].reshape(bd, 1, GVW),
      ps["mq"].reshape(bd, 1, QKW), mem_kt, mem_vt)


def _decode_scores_body(pt_ref, iq_ref, iw_ref, ikn_ref, *rest, pg, past):
    pages, sc_ref = rest[:pg], rest[pg]
    s = pl.program_id(1)
    qs = iq_ref[...]
    qs16 = qs.astype(BF16)
    iw = iw_ref[...]
    rows = [jnp.sum(jnp.maximum(_nn(qs16, pages[i][...].astype(BF16)), 0.0) * iw, axis=0, keepdims=True)
            for i in range(pg)]
    off = pl.multiple_of(s * (pg * PAGE), pg * PAGE)
    sc_ref[:, pl.ds(off, pg * PAGE)] = jnp.concatenate(rows, axis=1)

    @pl.when(s == pl.num_programs(1) - 1)
    def _():
        rn = jnp.maximum(jnp.sum(qs * ikn_ref[...], axis=-1, keepdims=True), 0.0)
        new = jnp.sum(rn * iw, axis=0, keepdims=True)
        lane = lax.broadcasted_iota(I32, (1, PAGE), 1)
        sc_ref[:, past:past + PAGE] = jnp.where(lane == 0, new, -jnp.inf)


def _decode_scores(ps, cache_ikt, page_table, pg):
    bd, n_pages = page_table.shape
    past = n_pages * PAGE
    total = past + PAGE
    pad = lambda a: jnp.pad(a, ((0, 0), (0, 16 - HEADS), (0, 0)))
    iq16 = pad(ps["iq"].reshape(bd, HEADS, HD))
    iw16 = pad((ps["tail"][:, TAIL_IW:TAIL_IW + HEADS] * (HEADS ** -0.5)).reshape(bd, HEADS, 1))
    page_spec = lambda i: pl.BlockSpec((None, None, HD, PAGE),
                                       lambda b, s, pt: (pt[b, s * pg + i], 0, 0, 0))
    grid_spec = pltpu.PrefetchScalarGridSpec(
        num_scalar_prefetch=1,
        grid=(bd, n_pages // pg),
        in_specs=[pl.BlockSpec((None, 16, HD), lambda b, s, pt: (b, 0, 0)),
                  pl.BlockSpec((None, 16, 1), lambda b, s, pt: (b, 0, 0)),
                  pl.BlockSpec((None, 1, HD), lambda b, s, pt: (b, 0, 0))]
                 + [page_spec(i) for i in range(pg)],
        out_specs=pl.BlockSpec((None, 1, total), lambda b, s, pt: (b, 0, 0)),
    )
    return pl.pallas_call(
        functools.partial(_decode_scores_body, pg=pg, past=past),
        grid_spec=grid_spec,
        out_shape=jax.ShapeDtypeStruct((bd, 1, total), F32),
        compiler_params=_params(("parallel", "arbitrary")),
        name="decode_scores",
    )(page_table, iq16, iw16, ps["ik"].reshape(bd, 1, HD), *([cache_ikt] * pg))


def _decode_select_body(sc_ref, sel_ref, *, topk, idx_bits):
    key = _sortable(sc_ref[...])
    idx = lax.broadcasted_iota(I32, key.shape, 1)
    kf = float(topk)

    def count(pred):
        return jnp.sum(pred(key, idx), axis=-1, keepdims=True)

    thr = _kth_threshold(count, kf)
    need = kf - count(lambda kk, ii: _ind(kk > thr))
    cut = _tie_cutoff(count, thr, need, idx_bits)
    sel_ref[...] = _selected(key, idx, thr, cut)


def _decode_select(scores, topk):
    bd, total = scores.shape
    return pl.pallas_call(
        functools.partial(_decode_select_body, topk=topk, idx_bits=max(1, (total - 1).bit_length())),
        out_shape=jax.ShapeDtypeStruct((bd, total), F32),
        compiler_params=pltpu.CompilerParams(vmem_limit_bytes=VMEM_LIMIT),
        name="decode_select",
    )(scores)


def _decode_attend_body(pt_ref, q_ref, kn_ref, vn_ref, sel_ref, seln_ref, *rest, pg):
    kpages, vpages = rest[:pg], rest[pg:2 * pg]
    o_ref, m_s, l_s, acc_s = rest[2 * pg:]
    s = pl.program_id(1)

    @pl.when(s == 0)
    def _():
        m_s[...] = jnp.full(m_s.shape, NEG, F32)
        l_s[...] = jnp.zeros(l_s.shape, F32)
        acc_s[...] = jnp.zeros(acc_s.shape, F32)

    bias = jnp.where(sel_ref[...] > 0.5, 0.0, NEG)
    for h in range(HEADS):
        qc = q_ref[h]
        sc = jnp.concatenate([jnp.sum(qc * kpages[i][h], axis=0, keepdims=True) for i in range(pg)],
                             axis=1) + bias
        m_old = m_s[h]
        m_new = jnp.maximum(m_old, jnp.max(sc, axis=1, keepdims=True))
        a = jnp.exp(m_old - m_new)
        p = jnp.exp(sc - m_new)
        l_s[h] = a * l_s[h] + jnp.sum(p, axis=1, keepdims=True)
        m_s[h] = m_new
        acc = acc_s[h] * a
        for i in range(pg):
            acc = acc + vpages[i][h] * p[:, i * PAGE:(i + 1) * PAGE]
        acc_s[h] = acc

    @pl.when(s == pl.num_programs(1) - 1)
    def _():
        sel_new = seln_ref[:, 0:1] > 0.5
        for h in range(HEADS):
            sn = jnp.where(sel_new, jnp.sum(q_ref[h] * kn_ref[h], axis=0, keepdims=True), NEG)
            m_old = m_s[h]
            m_new = jnp.maximum(m_old, sn)
            a = jnp.exp(m_old - m_new)
            pn = jnp.where(sel_new, jnp.exp(sn - m_new), 0.0)
            tot = a * jnp.sum(acc_s[h], axis=1, keepdims=True) + pn * vn_ref[h]
            o_ref[h] = tot / (a * l_s[h] + pn)


def _decode_attend(ps, sel, cache_kt, cache_vt, page_table, pg):
    bd, n_pages = page_table.shape
    past = n_pages * PAGE
    col = lambda a: a.reshape(bd, HEADS, HD, 1)
    cspec = pl.BlockSpec((None, HEADS, HD, 1), lambda b, s, pt: (b, 0, 0, 0))
    page_spec = lambda i: pl.BlockSpec((None, None, HEADS, HD, PAGE),
                                       lambda b, s, pt: (pt[b, s * pg + i], 0, 0, 0, 0))
    grid_spec = pltpu.PrefetchScalarGridSpec(
        num_scalar_prefetch=1,
        grid=(bd, n_pages // pg),
        in_specs=[cspec, cspec, cspec,
                  pl.BlockSpec((None, 1, pg * PAGE), lambda b, s, pt: (b, 0, s)),
                  pl.BlockSpec((None, 1, PAGE), lambda b, s, pt: (b, 0, past // PAGE))]
                 + [page_spec(i) for i in range(pg)] * 2,
        out_specs=cspec,
        scratch_shapes=[pltpu.VMEM((HEADS, 1, 1), F32), pltpu.VMEM((HEADS, 1, 1), F32),
                        pltpu.VMEM((HEADS, HD, PAGE), F32)],
    )
    sel3 = sel.reshape(bd, 1, past + PAGE)
    return pl.pallas_call(
        functools.partial(_decode_attend_body, pg=pg),
        grid_spec=grid_spec,
        out_shape=jax.ShapeDtypeStruct((bd, HEADS, HD, 1), F32),
        compiler_params=_params(("parallel", "arbitrary")),
        name="decode_attend",
    )(page_table, col(ps["q"]), col(ps["k"]), col(ps["v"]), sel3, sel3,
      *([cache_kt] * pg), *([cache_vt] * pg))


def _rope_tables(pos):
    inv = ROPE_THETA ** (-jnp.arange(0, HD, 2, dtype=F32) / HD)
    ang = pos.astype(F32)[:, None] * inv[None, :]
    cos, sin = jnp.cos(ang), jnp.sin(ang)
    zero = jnp.zeros_like(sin)
    tile = lambda a, b: jnp.tile(jnp.concatenate([a, b], axis=1), (1, HEADS))
    return tile(cos, cos), tile(-sin, zero), tile(zero, sin), cos.T, sin.T


def _pick_tile(n, pref):
    t = min(n, pref)
    while n % t:
        t //= 2
    return t


_PROMPT_OUTS = ("gq", "gk", "gv", "gvt", "gg", "ggate", "q", "k", "iq", "ik4", "kT", "vT", "ikT", "tailT",
                "dgate", "mq", "mgate")
_SAMPLE_OUTS = ("gq", "gk", "gv", "gg", "ggate", "q", "k", "v", "iq", "ik", "tail", "dgate", "mq", "mgate")


def kernel(x_prompt, x_sample, mem_prompt, state_gla, cache_k, cache_v, cache_ik, cache_mem_k, cache_mem_v,
           page_table, norm_in, w_in, w_gla_g2, b_gla_g, gla_onorm, mem_norm, w_mem_kv, w_out, norm_final):
    bp, tp, d = x_prompt.shape
    bs, ts, _ = x_sample.shape
    n_pages = page_table.shape[1]
    n_mem = mem_prompt.shape[1]
    assert w_in.shape[0] == 1 and ts == 1, "single layer, single decode token"
    past = n_pages * PAGE

    sizes = (QKW, QKW, GVW, GATE_RANK, GVW, QKW, QKW, QKW, QKW, HD, HEADS, QKW, QKW, QKW)
    offs = np.cumsum(sizes)[:-1].tolist()
    (w_gq, w_gk, w_gv, w_glr, w_ggate, w_q, w_k, w_v, w_iq, w_ik, w_iw, w_dgate, w_mq, w_mgate) = jnp.split(
        w_in[0], offs, axis=1)
    w_tail = jnp.concatenate([w_glr, w_iw, jnp.zeros((d, 128 - GATE_RANK - HEADS), F32)], axis=1)
    w_cat = jnp.concatenate([w_gq, w_gk, w_gv, w_ggate, w_q, w_k, w_v, w_iq, jnp.tile(w_ik, (1, HEADS)),
                             w_dgate, w_mq, w_mgate, w_tail], axis=1).astype(BF16)
    wt_cat = jnp.concatenate([w_gv, w_k, w_v, w_ik, w_tail], axis=1).T.astype(BF16)
    w2pad = jnp.zeros((128, QKW), F32).at[:GATE_RANK].set(w_gla_g2[0]).astype(BF16)
    weights = (norm_in[0].reshape(1, d), w_cat, wt_cat, w2pad, b_gla_g[0].reshape(1, QKW))
    wo = w_out[0].astype(BF16)
    onorm = gla_onorm[0].reshape(1, GLA_DV)
    nf = norm_final.reshape(1, d)

    xp = x_prompt.reshape(bp * tp, d)
    tm = _pick_tile(tp, 256)
    pp = _project(xp, weights, _rope_tables(jnp.arange(tp)), tm, _PROMPT_OUTS, tp)
    gla_o, st = _gla_prompt(pp, bp, tp, _pick_tile(tp, 128))
    dsa_o = _dsa_prompt(pp, bp, tp, _pick_tile(tp, 256))
    wm = w_mem_kv[0].astype(BF16)
    mk, mv, mkt, mvt = _mem_kv(mem_prompt.reshape(bp * n_mem, d), mem_norm[0].reshape(1, d), wm, wm.T, bp, n_mem)
    mem_o = _mem_attend_prompt(pp["mq"], mk, mv, bp, tp, n_mem, tm)
    y_prompt = _combine(gla_o, pp["ggate"], dsa_o, pp["dgate"], mem_o, pp["mgate"], xp, onorm, wo, nf, tm)

    xs = x_sample.reshape(bs, d)
    ps = _project(xs, weights, _rope_tables(jnp.full((bs,), past, I32)), bs, _SAMPLE_OUTS, bs)
    mem_t = lambda c: c[0].transpose(0, 2, 3, 1).reshape(bs, QKW, n_mem)
    gla_os, state_new, mem_os = _decode_misc(state_gla[0].reshape(bs, QKW, GLA_DV), ps,
                                             mem_t(cache_mem_k), mem_t(cache_mem_v))
    pg = _pick_tile(n_pages, 16)
    scores = _decode_scores(ps, cache_ik.transpose(0, 1, 3, 2), page_table, pg)
    sel = _decode_select(scores.reshape(bs, past + PAGE), min(TOPK_MAX, (past + ts) // 4))
    dsa_os = _decode_attend(ps, sel, cache_k.transpose(0, 1, 3, 4, 2), cache_v.transpose(0, 1, 3, 4, 2),
                            page_table, pg)
    y_sample = _combine(gla_os.reshape(bs, GVW), ps["ggate"], dsa_os.reshape(bs, QKW), ps["dgate"],
                        mem_os.reshape(bs, QKW), ps["mgate"], xs, onorm, wo, nf, bs)

    st5 = st.reshape(bp, HEADS, GLA_DV, HEADS, HD)
    gla_state_p = jnp.stack([st5[:, h, :, h, :] for h in range(HEADS)], axis=1).swapaxes(-1, -2)
    heads_t = lambda a, b, t: a.reshape(b, 1, HEADS, HD, t).transpose(0, 1, 4, 2, 3)
    return (y_prompt.reshape(bp, tp, d), y_sample.reshape(bs, ts, d),
            gla_state_p[None],
            heads_t(pp["kT"], bp, tp), heads_t(pp["vT"], bp, tp),
            pp["ikT"].reshape(bp, 1, HD, tp).transpose(0, 1, 3, 2),
            heads_t(mkt, bp, n_mem).swapaxes(0, 1), heads_t(mvt, bp, n_mem).swapaxes(0, 1),
            state_new.reshape(1, bs, HEADS, HD, GLA_DV),
            ps["k"].reshape(bs, 1, ts, HEADS, HD), ps["v"].reshape(bs, 1, ts, HEADS, HD),
            ps["ik"].reshape(bs, 1, ts, HD))
```
